```python
import math
import jax, jax.numpy as jnp
from jax import lax
import numpy as np

D_MODEL = 1024
BATCH = 32
SEQ = 2048
DEPTH = 4
DEC_BATCH = 16
DEC_SEQ = 2048
PAST_LEN = 128

GRID_W = 64
N_EVEN = (DEPTH + 1) // 2
N_ODD = DEPTH // 2
NA_HEADS = 8
NA_HEAD_DIM = 64
NA_WIDTH = NA_HEADS * NA_HEAD_DIM
NA_WIN_R = 8
NA_WIN_C = 16
SGU_WIDTH = 512
SGU_GROUPS = 4
SGU_GROUP_DIM = SGU_WIDTH // SGU_GROUPS
SGU_CHUNK = 128
MLA_HEADS = 8
MLA_Q_LORA = 256
MLA_KV_LORA = 128
MLA_NOPE = 64
MLA_ROPE = 32
MLA_V = 64
MLA_WIDTH = MLA_HEADS * MLA_V
MLA_Q_BLOCK = 128
ROPE_THETA = 10000.0
CONV_WIDTH = 512
CONV_K = 31
D_FF = 2816
FFN_CONV_K = 3
LN_EPS = 1e-5
RMS_EPS = 1e-6
ALPHA = (2 * DEPTH) ** 0.25
BETA = (8 * DEPTH) ** -0.25
EVEN_IN = 3 * NA_WIDTH + 2 * SGU_WIDTH
EVEN_MIX = NA_WIDTH + SGU_WIDTH
ODD_IN = MLA_Q_LORA + MLA_KV_LORA + MLA_ROPE + 2 * CONV_WIDTH
ODD_MIX = MLA_WIDTH + CONV_WIDTH

kernel_name = 'hybrid_natten_sgu_mla_conformer_encoder'


def layernorm(x, g, b):
    xf = x.astype(jnp.float32)
    mu = jnp.mean(xf, axis=-1, keepdims=True)
    var = jnp.mean(jnp.square(xf - mu), axis=-1, keepdims=True)
    y = (xf - mu) * lax.rsqrt(var + LN_EPS)
    return (y * g.astype(jnp.float32) + b.astype(jnp.float32)).astype(x.dtype)


def rmsnorm(x, g):
    xf = x.astype(jnp.float32)
    y = xf * lax.rsqrt(jnp.mean(jnp.square(xf), axis=-1, keepdims=True) + RMS_EPS)
    return (y * g.astype(jnp.float32)).astype(x.dtype)


def dwconv(x, w, b):
    K, C = w.shape
    y = lax.conv_general_dilated(x, w[:, None, :].astype(x.dtype), window_strides=(1,),
                                 padding=[(K // 2, K // 2)],
                                 dimension_numbers=('NWC', 'WIO', 'NWC'),
                                 feature_group_count=C)
    return y + b.astype(x.dtype)


def rope_tables(S, dtype):
    pos = jnp.arange(S, dtype=jnp.float32)
    inv = ROPE_THETA ** (-jnp.arange(0, MLA_ROPE, 2, dtype=jnp.float32) / MLA_ROPE)
    ang = pos[:, None] * inv[None, :]
    return jnp.cos(ang).astype(dtype), jnp.sin(ang).astype(dtype)


def apply_rope(x, cos, sin):
    x1, x2 = jnp.split(x, 2, axis=-1)
    return jnp.concatenate([x1 * cos - x2 * sin, x2 * cos + x1 * sin], axis=-1)


def neighbourhood_attention(q, k, v, rpb):
    B, S, H, dh = q.shape
    rows = S // GRID_W
    wr = min(NA_WIN_R, rows)
    r = jnp.arange(rows)
    r0 = jnp.clip(r - wr // 2, 0, rows - wr)
    dr = r0[:, None] + jnp.arange(wr)[None, :] - r[:, None]
    c = jnp.arange(GRID_W)
    c0 = jnp.clip(c - NA_WIN_C // 2, 0, GRID_W - NA_WIN_C)
    col_mask = (c[None, :] >= c0[:, None]) & (c[None, :] < c0[:, None] + NA_WIN_C)
    dc_idx = jnp.clip(c[None, :] - c[:, None], -(NA_WIN_C - 1), NA_WIN_C - 1) + NA_WIN_C - 1
    qg = q.reshape(B, rows, GRID_W, H, dh).transpose(1, 0, 2, 3, 4)
    kg = k.reshape(B, rows, GRID_W, H, dh)
    vg = v.reshape(B, rows, GRID_W, H, dh)
    scale = dh ** -0.5

    def row_block(args):
        q_r, r0_r, dr_r = args
        k_r = lax.dynamic_slice_in_dim(kg, r0_r, wr, axis=1)
        v_r = lax.dynamic_slice_in_dim(vg, r0_r, wr, axis=1)
        s = jnp.einsum('bqhd,bikhd->bhqik', q_r, k_r).astype(jnp.float32) * scale
        bias = rpb[:, (dr_r + NA_WIN_R - 1)[None, :, None], dc_idx[:, None, :]]
        s = jnp.where(col_mask[:, None, :], s + bias.astype(jnp.float32)[None], -jnp.inf)
        p = jax.nn.softmax(s.reshape(B, H, GRID_W, wr * GRID_W), axis=-1)
        p = p.reshape(s.shape).astype(v.dtype)
        return jnp.einsum('bhqik,bikhd->bqhd', p, v_r)

    out = lax.map(row_block, (qg, r0, dr))
    return out.transpose(1, 0, 2, 3, 4).reshape(B, S, H * dh)


def spatial_gating(u, v, ln_g, ln_b, w_s, b_s):
    B, S, _ = u.shape
    u = jax.nn.gelu(u)
    v = layernorm(jax.nn.gelu(v), ln_g, ln_b)
    vc = v.reshape(B, S // SGU_CHUNK, SGU_CHUNK, SGU_GROUPS, SGU_GROUP_DIM)
    mixed = jnp.einsum('gpq,bnqgc->bnpgc', w_s, vc) + b_s.T[None, None, :, :, None]
    return u * mixed.reshape(B, S, SGU_WIDTH)


def latent_attention(c_q, c_kv, k_rope_in, q_norm_g, w_uq, kv_norm_g, w_ukv, cos, sin):
    B, S, _ = c_q.shape
    q = (rmsnorm(c_q, q_norm_g) @ w_uq).reshape(B, S, MLA_HEADS, MLA_NOPE + MLA_ROPE)
    q_nope, q_rope = jnp.split(q, [MLA_NOPE], axis=-1)
    q_rope = apply_rope(q_rope, cos[:, None, :], sin[:, None, :])
    kv = (rmsnorm(c_kv, kv_norm_g) @ w_ukv).reshape(B, S, MLA_HEADS, MLA_NOPE + MLA_V)
    k_nope, v = jnp.split(kv, [MLA_NOPE], axis=-1)
    k_rope = apply_rope(k_rope_in, cos, sin)
    nb = S // MLA_Q_BLOCK
    qn_b = q_nope.reshape(B, nb, MLA_Q_BLOCK, MLA_HEADS, MLA_NOPE).transpose(1, 0, 2, 3, 4)
    qr_b = q_rope.reshape(B, nb, MLA_Q_BLOCK, MLA_HEADS, MLA_ROPE).transpose(1, 0, 2, 3, 4)
    scale = (MLA_NOPE + MLA_ROPE) ** -0.5

    def q_block(args):
        qn_i, qr_i = args
        s = (jnp.einsum('bqhd,bkhd->bhqk', qn_i, k_nope)
             + jnp.einsum('bqhr,bkr->bhqk', qr_i, k_rope)).astype(jnp.float32) * scale
        p = jax.nn.softmax(s, axis=-1).astype(v.dtype)
        return jnp.einsum('bhqk,bkhd->bqhd', p, v)

    o = lax.map(q_block, (qn_b, qr_b))
    return o.transpose(1, 0, 2, 3, 4).reshape(B, S, MLA_WIDTH)


def even_mixer(x, w_in, rpb, sgu_ln_g, sgu_ln_b, sgu_w, sgu_b, w_out):
    B, S, _ = x.shape
    h = x @ w_in
    q, k, v, u, g = jnp.split(h, [NA_WIDTH, 2 * NA_WIDTH, 3 * NA_WIDTH, 3 * NA_WIDTH + SGU_WIDTH], axis=-1)
    shp = (B, S, NA_HEADS, NA_HEAD_DIM)
    a = neighbourhood_attention(q.reshape(shp), k.reshape(shp), v.reshape(shp), rpb)
    b = spatial_gating(u, g, sgu_ln_g, sgu_ln_b, sgu_w, sgu_b)
    return jnp.concatenate([a, b], axis=-1) @ w_out


def odd_mixer(x, cos, sin, w_in, q_norm_g, w_uq, kv_norm_g, w_ukv,
              conv_w, conv_b, conv_ln_g, conv_ln_b, w_out):
    h = x @ w_in
    o1 = MLA_Q_LORA
    o2 = o1 + MLA_KV_LORA
    o3 = o2 + MLA_ROPE
    o4 = o3 + CONV_WIDTH
    c_q, c_kv, k_r, glu_a, glu_g = jnp.split(h, [o1, o2, o3, o4], axis=-1)
    c = latent_attention(c_q, c_kv, k_r, q_norm_g, w_uq, kv_norm_g, w_ukv, cos, sin)
    d = glu_a * jax.nn.sigmoid(glu_g)
    d = jax.nn.silu(layernorm(dwconv(d, conv_w, conv_b), conv_ln_g, conv_ln_b))
    return jnp.concatenate([c, d], axis=-1) @ w_out


def conv_ffn(x, w_up, conv_w, conv_b, w_down):
    h = dwconv(x @ w_up, conv_w, conv_b)
    gate, val = jnp.split(h, 2, axis=-1)
    return (jax.nn.silu(gate) * val) @ w_down


def trunk(x, w_in_even, rpb, sgu_ln_g, sgu_ln_b, sgu_w, sgu_b, w_out_even,
          w_in_odd, q_norm_g, w_uq, kv_norm_g, w_ukv, conv_w, conv_b, conv_ln_g, conv_ln_b,
          w_out_odd, ffn_w_up, ffn_conv_w, ffn_conv_b, ffn_w_down, ln1_g, ln1_b, ln2_g, ln2_b):
    cos, sin = rope_tables(x.shape[1], x.dtype)
    for l in range(DEPTH):
        i = l // 2
        if l % 2 == 0:
            m = even_mixer(x, w_in_even[i], rpb[i], sgu_ln_g[i], sgu_ln_b[i], sgu_w[i], sgu_b[i],
                           w_out_even[i])
        else:
            m = odd_mixer(x, cos, sin, w_in_odd[i], q_norm_g[i], w_uq[i], kv_norm_g[i], w_ukv[i],
                          conv_w[i], conv_b[i], conv_ln_g[i], conv_ln_b[i], w_out_odd[i])
        x = layernorm(ALPHA * x + m, ln1_g[l], ln1_b[l])
        f = conv_ffn(x, ffn_w_up[l], ffn_conv_w[l], ffn_conv_b[l], ffn_w_down[l])
        x = layernorm(ALPHA * x + f, ln2_g[l], ln2_b[l])
    return x


def setup_inputs(seed: int = 0) -> dict:
    key = jax.random.key(seed)
    ks = jax.random.split(key, 32)
    f32 = jnp.float32

    def nrm(k, shape, scale):
        return jax.random.normal(k, shape, f32) * scale

    def gain(k, shape):
        return 1.0 + 0.01 * jax.random.normal(k, shape, f32)

    return {
        'x_prompt': nrm(ks[0], (BATCH, SEQ, D_MODEL), 1.0),
        'x_sample': nrm(ks[1], (DEC_BATCH, DEC_SEQ, D_MODEL), 1.0),
        'w_in_even': nrm(ks[2], (N_EVEN, D_MODEL, EVEN_IN), D_MODEL ** -0.5),
        'rpb': nrm(ks[3], (N_EVEN, NA_HEADS, 2 * NA_WIN_R - 1, 2 * NA_WIN_C - 1), 0.05),
        'sgu_ln_g': gain(ks[4], (N_EVEN, SGU_WIDTH)),
        'sgu_ln_b': nrm(ks[5], (N_EVEN, SGU_WIDTH), 0.01),
        'sgu_w': nrm(ks[6], (N_EVEN, SGU_GROUPS, SGU_CHUNK, SGU_CHUNK), SGU_CHUNK ** -0.5),
        'sgu_b': gain(ks[7], (N_EVEN, SGU_GROUPS, SGU_CHUNK)),
        'w_out_even': nrm(ks[8], (N_EVEN, EVEN_MIX, D_MODEL), EVEN_MIX ** -0.5 * BETA),
        'w_in_odd': nrm(ks[9], (N_ODD, D_MODEL, ODD_IN), D_MODEL ** -0.5),
        'q_norm_g': gain(ks[10], (N_ODD, MLA_Q_LORA)),
        'w_uq': nrm(ks[11], (N_ODD, MLA_Q_LORA, MLA_HEADS * (MLA_NOPE + MLA_ROPE)), MLA_Q_LORA ** -0.5),
        'kv_norm_g': gain(ks[12], (N_ODD, MLA_KV_LORA)),
        'w_ukv': nrm(ks[13], (N_ODD, MLA_KV_LORA, MLA_HEADS * (MLA_NOPE + MLA_V)), MLA_KV_LORA ** -0.5),
        'conv_w': nrm(ks[14], (N_ODD, CONV_K, CONV_WIDTH), CONV_K ** -0.5),
        'conv_b': nrm(ks[15], (N_ODD, CONV_WIDTH), 0.01),
        'conv_ln_g': gain(ks[16], (N_ODD, CONV_WIDTH)),
        'conv_ln_b': nrm(ks[17], (N_ODD, CONV_WIDTH), 0.01),
        'w_out_odd': nrm(ks[18], (N_ODD, ODD_MIX, D_MODEL), ODD_MIX ** -0.5 * BETA),
        'ffn_w_up': nrm(ks[19], (DEPTH, D_MODEL, 2 * D_FF), D_MODEL ** -0.5),
        'ffn_conv_w': nrm(ks[20], (DEPTH, FFN_CONV_K, 2 * D_FF), FFN_CONV_K ** -0.5),
        'ffn_conv_b': nrm(ks[21], (DEPTH, 2 * D_FF), 0.01),
        'ffn_w_down': nrm(ks[22], (DEPTH, D_FF, D_MODEL), D_FF ** -0.5 * BETA),
        'ln1_g': gain(ks[23], (DEPTH, D_MODEL)),
        'ln1_b': nrm(ks[24], (DEPTH, D_MODEL), 0.01),
        'ln2_g': gain(ks[25], (DEPTH, D_MODEL)),
        'ln2_b': nrm(ks[26], (DEPTH, D_MODEL), 0.01),
    }


def reference(x_prompt, x_sample, w_in_even, rpb, sgu_ln_g, sgu_ln_b, sgu_w, sgu_b, w_out_even,
              w_in_odd, q_norm_g, w_uq, kv_norm_g, w_ukv, conv_w, conv_b, conv_ln_g, conv_ln_b,
              w_out_odd, ffn_w_up, ffn_conv_w, ffn_conv_b, ffn_w_down, ln1_g, ln1_b, ln2_g, ln2_b):
    y_prompt = trunk(x_prompt, w_in_even, rpb, sgu_ln_g, sgu_ln_b, sgu_w, sgu_b, w_out_even,
                     w_in_odd, q_norm_g, w_uq, kv_norm_g, w_ukv, conv_w, conv_b, conv_ln_g, conv_ln_b,
                     w_out_odd, ffn_w_up, ffn_conv_w, ffn_conv_b, ffn_w_down, ln1_g, ln1_b, ln2_g, ln2_b)
    y_sample = trunk(x_sample, w_in_even, rpb, sgu_ln_g, sgu_ln_b, sgu_w, sgu_b, w_out_even,
                     w_in_odd, q_norm_g, w_uq, kv_norm_g, w_ukv, conv_w, conv_b, conv_ln_g, conv_ln_b,
                     w_out_odd, ffn_w_up, ffn_conv_w, ffn_conv_b, ffn_w_down, ln1_g, ln1_b, ln2_g, ln2_b)
    return (y_prompt, y_sample)
```

```python
import functools

import numpy as np
import jax
import jax.numpy as jnp
from jax import lax
from jax.experimental import pallas as pl
from jax.experimental.pallas import tpu as pltpu

F32 = jnp.float32
BF16 = jnp.bfloat16

D_MODEL = 1024
SEQ = 2048
DEPTH = 4
GRID_W = 64
ROWS = SEQ // GRID_W
NA_HEADS = 8
NA_HEAD_DIM = 64
NA_WIDTH = 512
NA_WIN_R = 8
NA_WIN_C = 16
SGU_WIDTH = 512
SGU_GROUPS = 4
SGU_CHUNK = 128
MLA_HEADS = 8
MLA_Q_LORA = 256
MLA_KV_LORA = 128
MLA_NOPE = 64
MLA_ROPE = 32
MLA_V = 64
MLA_WIDTH = 512
ROPE_THETA = 10000.0
CONV_WIDTH = 512
CONV_K = 31
D_FF = 2816
LN_EPS = 1e-5
RMS_EPS = 1e-6
ALPHA = (2 * DEPTH) ** 0.25

VMEM_LIMIT_BYTES = 56 * 1024 * 1024
LANES = 128
HALO = 16
TM = 512
TILES_PER_SEQ = SEQ // TM
FF_CHUNK = 256
N_FF_CHUNKS = D_FF // FF_CHUNK
NA_QROWS = 4
NA_KROWS = 12
NA_QBLK = NA_QROWS * GRID_W
NA_KBLK = NA_KROWS * GRID_W
NA_BLOCKS = ROWS // NA_QROWS
MLA_QBLK = 256
NEG = -1e30


def _params(n_axes=1):
    return pltpu.CompilerParams(dimension_semantics=("arbitrary",) * n_axes,
                                vmem_limit_bytes=VMEM_LIMIT_BYTES)


def _const_spec(shape):
    nd = len(shape)
    return pl.BlockSpec(shape, lambda *_: (0,) * nd, pipeline_mode=pl.Buffered(1))


def _layernorm(z, g, b):
    mu = jnp.mean(z, axis=-1, keepdims=True)
    zc = z - mu
    var = jnp.mean(zc * zc, axis=-1, keepdims=True)
    return zc * lax.rsqrt(var + LN_EPS) * g + b


def _rmsnorm(z, g):
    return z * lax.rsqrt(jnp.mean(z * z, axis=-1, keepdims=True) + RMS_EPS) * g


def _sigmoid(z):
    return 1.0 / (1.0 + jnp.exp(-z))


def _dot(a, b):
    return jnp.dot(a, b, preferred_element_type=F32)


def _dot_nt(a, b):
    return lax.dot_general(a, b, (((1,), (1,)), ((), ())), preferred_element_type=F32)


def _even_in_kernel(x_ref, w_ref, g_ref, b_ref, qkv_ref, u_ref, v_ref):
    xb = x_ref[...].astype(BF16)
    qkv_ref[...] = _dot(xb, w_ref[:, :3 * NA_WIDTH]).astype(BF16)
    ug = _dot(xb, w_ref[:, 3 * NA_WIDTH:])
    u_ref[...] = jax.nn.gelu(ug[:, :SGU_WIDTH])
    v = _layernorm(jax.nn.gelu(ug[:, SGU_WIDTH:]), g_ref[...], b_ref[...])
    v_ref[...] = v.astype(BF16)


def _even_in(x, w, ln_g, ln_b):
    T = x.shape[0]
    tok = lambda n: pl.BlockSpec((TM, n), lambda i: (i, 0))
    return pl.pallas_call(
        _even_in_kernel,
        grid=(T // TM,),
        in_specs=[tok(D_MODEL), _const_spec(w.shape), _const_spec(ln_g.shape), _const_spec(ln_b.shape)],
        out_specs=[tok(3 * NA_WIDTH), tok(SGU_WIDTH), tok(SGU_WIDTH)],
        out_shape=[jax.ShapeDtypeStruct((T, 3 * NA_WIDTH), BF16),
                   jax.ShapeDtypeStruct((T, SGU_WIDTH), F32),
                   jax.ShapeDtypeStruct((T, SGU_WIDTH), BF16)],
        compiler_params=_params(),
        name="even_in",
    )(x, w, ln_g, ln_b)


def _nbr_attn_kernel(q_ref, k_ref, v_ref, bias_ref, o_ref):
    lane = lax.broadcasted_iota(jnp.int32, (1, LANES), 1)
    head_lanes = (lane < NA_HEAD_DIM, lane >= NA_HEAD_DIM)

    def block(m, carry):
        key_row0 = jnp.clip(NA_QROWS * m - NA_WIN_R // 2, 0, ROWS - NA_KROWS)
        cls = jnp.where(m == 0, 0, jnp.where(m == NA_BLOCKS - 1, 2, 1))
        q = q_ref[0, pl.ds(pl.multiple_of(m * NA_QBLK, NA_QBLK), NA_QBLK), :]
        ks = pl.ds(pl.multiple_of(key_row0 * GRID_W, GRID_W), NA_KBLK)
        k = k_ref[0, ks, :]
        v = v_ref[0, ks, :]
        out = jnp.zeros((NA_QBLK, LANES), F32)
        for h in range(2):
            qh = jnp.where(head_lanes[h], q, jnp.zeros_like(q))
            s = _dot_nt(qh, k) + bias_ref[cls, h]
            p = jnp.exp(s - jnp.max(s, axis=-1, keepdims=True))
            l = jnp.sum(p, axis=-1, keepdims=True)
            vh = jnp.where(head_lanes[h], v, jnp.zeros_like(v))
            out = out + _dot(p.astype(BF16), vh) * (1.0 / l)
        o_ref[0, pl.ds(pl.multiple_of(m * NA_QBLK, NA_QBLK), NA_QBLK), :] = out.astype(BF16)
        return carry

    lax.fori_loop(0, NA_BLOCKS, block, 0)


def _nbr_attn(qkv, bias):
    B = qkv.shape[0]
    pairs = NA_HEADS // 2
    col = lambda off: pl.BlockSpec((1, SEQ, LANES), lambda p, b: (b, 0, off + p))
    return pl.pallas_call(
        _nbr_attn_kernel,
        grid=(pairs, B),
        in_specs=[col(0), col(pairs), col(2 * pairs),
                  pl.BlockSpec((3, 2, NA_QBLK, NA_KBLK), lambda p, b: (0, p, 0, 0))],
        out_specs=col(0),
        out_shape=jax.ShapeDtypeStruct((B, SEQ, NA_WIDTH), BF16),
        compiler_params=_params(2),
        name="nbr_attn",
    )(qkv, qkv, qkv, bias)


def _out_proj_ln(x, mixer_lo, mixer_hi, w_ref, g_ref, b_ref):
    half = w_ref.shape[0] // 2
    m = _dot(mixer_lo, w_ref[:half, :]) + _dot(mixer_hi, w_ref[half:, :])
    return _layernorm(ALPHA * x + m, g_ref[...], b_ref[...])


def _even_out_kernel(a_ref, u_ref, v_ref, x_ref, sw_ref, sb_ref, w_ref, g_ref, b_ref, o_ref, gate_scr):
    for c in range(TM // SGU_CHUNK):
        rows = slice(c * SGU_CHUNK, (c + 1) * SGU_CHUNK)
        for g in range(SGU_GROUPS):
            cols = slice(g * LANES, (g + 1) * LANES)
            mixed = _dot(sw_ref[g], v_ref[rows, cols]) + sb_ref[g]
            gate_scr[rows, cols] = (u_ref[rows, cols] * mixed).astype(BF16)
    o_ref[...] = _out_proj_ln(x_ref[...], a_ref[...], gate_scr[...], w_ref, g_ref, b_ref)


def _even_out(a, u, v, x, sgu_w, sgu_bias, w_out, ln_g, ln_b):
    T = x.shape[0]
    tok = lambda n: pl.BlockSpec((TM, n), lambda i: (i, 0))
    consts = (sgu_w, sgu_bias, w_out, ln_g, ln_b)
    return pl.pallas_call(
        _even_out_kernel,
        grid=(T // TM,),
        in_specs=[tok(NA_WIDTH), tok(SGU_WIDTH), tok(SGU_WIDTH), tok(D_MODEL)]
                 + [_const_spec(c.shape) for c in consts],
        out_specs=tok(D_MODEL),
        out_shape=jax.ShapeDtypeStruct((T, D_MODEL), F32),
        scratch_shapes=[pltpu.VMEM((TM, SGU_WIDTH), BF16)],
        compiler_params=_params(),
        name="even_out",
    )(a, u, v, x, *consts)


def _odd_in_kernel(x_ref, w_ref, qg_ref, wq_ref, kg_ref, wkv_ref, csq_ref, csk_ref,
                   q_ref, kn_ref, kr_ref, v_ref, d_ref):
    xb = x_ref[...].astype(BF16)
    h = _dot(xb, w_ref[...])
    o1 = MLA_Q_LORA
    o2 = o1 + MLA_KV_LORA
    o3 = o2 + LANES
    o4 = o3 + CONV_WIDTH

    cq = _rmsnorm(h[:, :o1], qg_ref[...]).astype(BF16)
    qf = _dot(cq, wq_ref[...])
    csq = csq_ref[...]
    for p in range(MLA_HEADS // 2):
        base = 2 * LANES * p
        q_ref[:, base:base + LANES] = qf[:, base:base + LANES].astype(BF16)
        prod = qf[:, base + LANES:base + 2 * LANES] * csq
        q_ref[:, base + LANES:base + 2 * LANES] = (prod + pltpu.roll(prod, 64, 1)).astype(BF16)

    ckv = _rmsnorm(h[:, o1:o2], kg_ref[...]).astype(BF16)
    kv = _dot(ckv, wkv_ref[...])
    kn_ref[...] = kv[:, :MLA_WIDTH].astype(BF16)
    v_ref[...] = kv[:, MLA_WIDTH:].astype(BF16)

    prod = h[:, o2:o3] * csk_ref[...]
    roped = prod + pltpu.roll(prod, LANES - MLA_ROPE, 1)
    lane = lax.broadcasted_iota(jnp.int32, (1, LANES), 1)
    kr_ref[...] = jnp.where(lane < MLA_ROPE, roped, pltpu.roll(roped, MLA_ROPE, 1)).astype(BF16)

    d_ref[...] = h[:, o3:o4] * _sigmoid(h[:, o4:])


def _odd_in(x, w, q_g, wq, kv_g, wkv, csq, csk):
    T = x.shape[0]
    tok = lambda n: pl.BlockSpec((TM, n), lambda i: (i, 0))
    pos = pl.BlockSpec((TM, LANES), lambda i: (i % TILES_PER_SEQ, 0))
    return pl.pallas_call(
        _odd_in_kernel,
        grid=(T // TM,),
        in_specs=[tok(D_MODEL), _const_spec(w.shape), _const_spec(q_g.shape), _const_spec(wq.shape),
                  _const_spec(kv_g.shape), _const_spec(wkv.shape), pos, pos],
        out_specs=[tok(2 * MLA_WIDTH), tok(MLA_WIDTH), tok(LANES), tok(MLA_WIDTH), tok(CONV_WIDTH)],
        out_shape=[jax.ShapeDtypeStruct((T, 2 * MLA_WIDTH), BF16),
                   jax.ShapeDtypeStruct((T, MLA_WIDTH), BF16),
                   jax.ShapeDtypeStruct((T, LANES), BF16),
                   jax.ShapeDtypeStruct((T, MLA_WIDTH), BF16),
                   jax.ShapeDtypeStruct((T, CONV_WIDTH), F32)],
        compiler_params=_params(),
        name="odd_in",
    )(x, w, q_g, wq, kv_g, wkv, csq, csk)


def _latent_attn_kernel(q_ref, kn_ref, kr_ref, v_ref, o_ref, k_scr, v_scr):
    lane = lax.broadcasted_iota(jnp.int32, (1, LANES), 1)
    lane2 = lax.broadcasted_iota(jnp.int32, (1, 2 * LANES), 1)
    k_scr[:, :LANES] = kn_ref[0]
    k_scr[:, LANES:] = kr_ref[0]
    v = v_ref[0]
    v_lanes = (lane < MLA_V, lane >= MLA_V)
    q_lanes = []
    for h in range(2):
        v_scr[h] = jnp.where(v_lanes[h], v, jnp.zeros_like(v))
        nope = (lane2 >= h * MLA_NOPE) & (lane2 < (h + 1) * MLA_NOPE)
        rope = (lane2 >= LANES + h * MLA_ROPE) & (lane2 < LANES + (h + 1) * MLA_ROPE)
        q_lanes.append(nope | rope)

    def block(i, carry):
        rows = pl.ds(pl.multiple_of(i * MLA_QBLK, MLA_QBLK), MLA_QBLK)
        q = q_ref[0, rows, :]
        out = jnp.zeros((MLA_QBLK, LANES), F32)
        for h in range(2):
            qh = jnp.where(q_lanes[h], q, jnp.zeros_like(q))
            s = _dot_nt(qh, k_scr[...])
            p = jnp.exp(s - jnp.max(s, axis=-1, keepdims=True))
            l = jnp.sum(p, axis=-1, keepdims=True)
            out = out + _dot(p.astype(BF16), v_scr[h]) * (1.0 / l)
        o_ref[0, rows, :] = out.astype(BF16)
        return carry

    lax.fori_loop(0, SEQ // MLA_QBLK, block, 0)


def _latent_attn(q, kn, kr, v):
    B = q.shape[0]
    pairs = MLA_HEADS // 2
    col = lambda width: pl.BlockSpec((1, SEQ, width), lambda b, p: (b, 0, p))
    return pl.pallas_call(
        _latent_attn_kernel,
        grid=(B, pairs),
        in_specs=[col(2 * LANES), col(LANES), pl.BlockSpec((1, SEQ, LANES), lambda b, p: (b, 0, 0)), col(LANES)],
        out_specs=col(LANES),
        out_shape=jax.ShapeDtypeStruct((B, SEQ, MLA_WIDTH), BF16),
        scratch_shapes=[pltpu.VMEM((SEQ, 2 * LANES), BF16), pltpu.VMEM((2, SEQ, LANES), BF16)],
        compiler_params=_params(2),
        name="latent_attn",
    )(q, kn, kr, v)


def _fill_halo_buffer(buf, prev_ref, cur_ref, next_ref, dtype):
    i = pl.program_id(0) % TILES_PER_SEQ
    prev = prev_ref[...]
    nxt = next_ref[...]
    buf[:HALO, :] = jnp.where(i == 0, jnp.zeros_like(prev), prev).astype(dtype)
    buf[HALO:HALO + TM, :] = cur_ref[...].astype(dtype)
    buf[HALO + TM:, :] = jnp.where(i == TILES_PER_SEQ - 1, jnp.zeros_like(nxt), nxt).astype(dtype)


def _halo_specs(n):
    per = TM // HALO
    last = lambda T: T // HALO - 1
    return (lambda T: pl.BlockSpec((HALO, n), lambda i: (jnp.maximum(i * per - 1, 0), 0)),
            lambda T: pl.BlockSpec((HALO, n), lambda i: (jnp.minimum((i + 1) * per, last(T)), 0)))


def _odd_out_kernel(c_ref, dp_ref, d_ref, dn_ref, x_ref, cw_ref, cb_ref, cg_ref, cbeta_ref,
                    w_ref, g_ref, b_ref, o_ref, buf, conv_scr):
    _fill_halo_buffer(buf, dp_ref, d_ref, dn_ref, F32)
    pad = CONV_K // 2
    for c in range(CONV_WIDTH // LANES):
        cols = slice(c * LANES, (c + 1) * LANES)
        acc = jnp.zeros((TM, LANES), F32) + cb_ref[:, cols]
        for k in range(CONV_K):
            acc = acc + buf[HALO - pad + k:HALO - pad + k + TM, cols] * cw_ref[k:k + 1, cols]
        conv_scr[:, cols] = acc
    y = _layernorm(conv_scr[...], cg_ref[...], cbeta_ref[...])
    y = (y * _sigmoid(y)).astype(BF16)
    o_ref[...] = _out_proj_ln(x_ref[...], c_ref[...], y, w_ref, g_ref, b_ref)


def _odd_out(c, d, x, conv_w, conv_b, conv_g, conv_beta, w_out, ln_g, ln_b):
    T = x.shape[0]
    tok = lambda n: pl.BlockSpec((TM, n), lambda i: (i, 0))
    prev, nxt = _halo_specs(CONV_WIDTH)
    consts = (conv_w, conv_b, conv_g, conv_beta, w_out, ln_g, ln_b)
    return pl.pallas_call(
        _odd_out_kernel,
        grid=(T // TM,),
        in_specs=[tok(MLA_WIDTH), prev(T), tok(CONV_WIDTH), nxt(T), tok(D_MODEL)]
                 + [_const_spec(a.shape) for a in consts],
        out_specs=tok(D_MODEL),
        out_shape=jax.ShapeDtypeStruct((T, D_MODEL), F32),
        scratch_shapes=[pltpu.VMEM((TM + 2 * HALO, CONV_WIDTH), F32), pltpu.VMEM((TM, CONV_WIDTH), F32)],
        compiler_params=_params(),
        name="odd_out",
    )(c, d, d, d, x, *consts)


def _conv_ffn_kernel(xp_ref, x_ref, xn_ref, wu_ref, cw_ref, cb_ref, wd_ref, g_ref, b_ref, o_ref,
                     lhs, h_scr, acc):
    _fill_halo_buffer(lhs, xp_ref, x_ref, xn_ref, BF16)
    acc[...] = jnp.zeros_like(acc)

    def chunk(j, carry):
        h_scr[...] = _dot(lhs[...], wu_ref[j])
        cw = cw_ref[j]
        conv = (h_scr[HALO - 1:HALO - 1 + TM, :] * cw[0:1, :]
                + h_scr[HALO:HALO + TM, :] * cw[1:2, :]
                + h_scr[HALO + 1:HALO + 1 + TM, :] * cw[2:3, :]
                + cb_ref[j])
        gate = conv[:, :FF_CHUNK]
        act = (gate * _sigmoid(gate) * conv[:, FF_CHUNK:]).astype(BF16)
        acc[...] += _dot(act, wd_ref[j])
        return carry

    lax.fori_loop(0, N_FF_CHUNKS, chunk, 0)
    o_ref[...] = _layernorm(ALPHA * x_ref[...] + acc[...], g_ref[...], b_ref[...])


def _conv_ffn(x, w_up, conv_w, conv_b, w_down, ln_g, ln_b):
    T = x.shape[0]
    tok = pl.BlockSpec((TM, D_MODEL), lambda i: (i, 0))
    prev, nxt = _halo_specs(D_MODEL)
    consts = (w_up, conv_w, conv_b, w_down, ln_g, ln_b)
    return pl.pallas_call(
        _conv_ffn_kernel,
        grid=(T // TM,),
        in_specs=[prev(T), tok, nxt(T)] + [_const_spec(a.shape) for a in consts],
        out_specs=tok,
        out_shape=jax.ShapeDtypeStruct((T, D_MODEL), F32),
        scratch_shapes=[pltpu.VMEM((TM + 2 * HALO, D_MODEL), BF16),
                        pltpu.VMEM((TM + 2 * HALO, 2 * FF_CHUNK), F32),
                        pltpu.VMEM((TM, D_MODEL), F32)],
        compiler_params=_params(),
        name="conv_ffn",
    )(x, x, x, *consts)


def _row(v):
    return v.reshape(1, -1).astype(F32)


def _nbr_bias(rpb):
    c = np.arange(GRID_W)
    c0 = np.clip(c - NA_WIN_C // 2, 0, GRID_W - NA_WIN_C)
    col_ok = (c[None, :] >= c0[:, None]) & (c[None, :] < c0[:, None] + NA_WIN_C)
    dc = np.clip(c[None, :] - c[:, None], -(NA_WIN_C - 1), NA_WIN_C - 1) + NA_WIN_C - 1
    row_tab = jnp.where(col_ok[None, None], rpb[:, :, dc], NEG)
    row_tab = jnp.concatenate([row_tab, jnp.full_like(row_tab[:, :1], NEG)], axis=1)
    masked = 2 * NA_WIN_R - 1
    sel = np.full((3, NA_QROWS, NA_KROWS), masked)
    for cls, m in enumerate((0, 1, NA_BLOCKS - 1)):
        key_row0 = min(max(NA_QROWS * m - NA_WIN_R // 2, 0), ROWS - NA_KROWS)
        for i in range(NA_QROWS):
            qr = NA_QROWS * m + i
            r0 = min(max(qr - NA_WIN_R // 2, 0), ROWS - NA_WIN_R)
            for j in range(NA_KROWS):
                kr = key_row0 + j
                if r0 <= kr < r0 + NA_WIN_R:
                    sel[cls, i, j] = kr - qr + NA_WIN_R - 1
    b = row_tab[:, sel]
    b = b.transpose(1, 0, 2, 4, 3, 5)
    return b.reshape(3, NA_HEADS, NA_QBLK, NA_KBLK).astype(F32)


def _rope_tables():
    pos = jnp.arange(SEQ, dtype=F32)
    inv = ROPE_THETA ** (-jnp.arange(0, MLA_ROPE, 2, dtype=F32) / MLA_ROPE)
    ang = pos[:, None] * inv[None, :]
    cos, sin = jnp.cos(ang), jnp.sin(ang)
    csq = jnp.concatenate([jnp.tile(cos, (1, 4)), jnp.tile(sin, (1, 4))], axis=1)
    csk = jnp.concatenate([jnp.tile(cos, (1, 2)), jnp.tile(sin, (1, 2)),
                           jnp.zeros((SEQ, LANES - 2 * MLA_ROPE), F32)], axis=1)
    return csq, csk


def _rot_cols(w):
    half = w.shape[-1] // 2
    return jnp.concatenate([-w[..., half:], w[..., :half]], axis=-1)


def _odd_weights(w_in, w_uq, w_ukv):
    o1 = MLA_Q_LORA
    o2 = o1 + MLA_KV_LORA
    o3 = o2 + MLA_ROPE
    k_r = w_in[:, o2:o3]
    k_blk = jnp.concatenate([k_r, _rot_cols(k_r), jnp.zeros((D_MODEL, LANES - 2 * MLA_ROPE), F32)], axis=1)
    w_in2 = jnp.concatenate([w_in[:, :o2], k_blk, w_in[:, o3:]], axis=1)

    scale = (MLA_NOPE + MLA_ROPE) ** -0.5
    wq = (w_uq * scale).reshape(MLA_Q_LORA, MLA_HEADS, MLA_NOPE + MLA_ROPE)
    nope, rope = wq[:, :, :MLA_NOPE], wq[:, :, MLA_NOPE:]
    rot = _rot_cols(rope)
    blocks = []
    for p in range(MLA_HEADS // 2):
        h0, h1 = 2 * p, 2 * p + 1
        blocks += [nope[:, h0], nope[:, h1], rope[:, h0], rope[:, h1], rot[:, h0], rot[:, h1]]
    wq2 = jnp.concatenate(blocks, axis=1)

    wkv = w_ukv.reshape(MLA_KV_LORA, MLA_HEADS, MLA_NOPE + MLA_V)
    wkv2 = jnp.concatenate([wkv[:, :, :MLA_NOPE].reshape(MLA_KV_LORA, -1),
                            wkv[:, :, MLA_NOPE:].reshape(MLA_KV_LORA, -1)], axis=1)
    return w_in2.astype(BF16), wq2.astype(BF16), wkv2.astype(BF16)


def _ffn_weights(w_up, conv_w, conv_b, w_down):
    def chunked(a):
        g = a[..., :D_FF].reshape(a.shape[:-1] + (N_FF_CHUNKS, FF_CHUNK))
        v = a[..., D_FF:].reshape(a.shape[:-1] + (N_FF_CHUNKS, FF_CHUNK))
        return jnp.moveaxis(jnp.concatenate([g, v], axis=-1), -2, 0)
    return (chunked(w_up).astype(BF16), chunked(conv_w).astype(F32), chunked(conv_b[None, :]).astype(F32),
            w_down.reshape(N_FF_CHUNKS, FF_CHUNK, D_MODEL).astype(BF16))


def _trunk(x, w_in_even, rpb, sgu_ln_g, sgu_ln_b, sgu_w, sgu_b, w_out_even,
           w_in_odd, q_norm_g, w_uq, kv_norm_g, w_ukv, conv_w, conv_b, conv_ln_g, conv_ln_b,
           w_out_odd, ffn_w_up, ffn_conv_w, ffn_conv_b, ffn_w_down, ln1_g, ln1_b, ln2_g, ln2_b):
    B = x.shape[0]
    T = B * SEQ
    x = x.reshape(T, D_MODEL)
    csq, csk = _rope_tables()
    for l in range(DEPTH):
        i = l // 2
        if l % 2 == 0:
            w_in = jnp.concatenate([w_in_even[i][:, :NA_WIDTH] * NA_HEAD_DIM ** -0.5,
                                    w_in_even[i][:, NA_WIDTH:]], axis=1).astype(BF16)
            qkv, u, v = _even_in(x, w_in, _row(sgu_ln_g[i]), _row(sgu_ln_b[i]))
            a = _nbr_attn(qkv.reshape(B, SEQ, 3 * NA_WIDTH), _nbr_bias(rpb[i])).reshape(T, NA_WIDTH)
            sgu_bias = jnp.broadcast_to(sgu_b[i][:, :, None], (SGU_GROUPS, SGU_CHUNK, LANES)).astype(F32)
            x = _even_out(a, u, v, x, sgu_w[i].astype(BF16), sgu_bias, w_out_even[i].astype(BF16),
                          _row(ln1_g[l]), _row(ln1_b[l]))
        else:
            w_in, wq, wkv = _odd_weights(w_in_odd[i], w_uq[i], w_ukv[i])
            q, kn, kr, v, d = _odd_in(x, w_in, _row(q_norm_g[i]), wq, _row(kv_norm_g[i]), wkv, csq, csk)
            c = _latent_attn(q.reshape(B, SEQ, -1), kn.reshape(B, SEQ, -1), kr.reshape(B, SEQ, -1),
                             v.reshape(B, SEQ, -1)).reshape(T, MLA_WIDTH)
            cw = jnp.concatenate([conv_w[i], jnp.zeros((1, CONV_WIDTH), F32)], axis=0)
            x = _odd_out(c, d, x, cw, _row(conv_b[i]), _row(conv_ln_g[i]), _row(conv_ln_b[i]),
                         w_out_odd[i].astype(BF16), _row(ln1_g[l]), _row(ln1_b[l]))
        wu, fcw, fcb, wd = _ffn_weights(ffn_w_up[l], ffn_conv_w[l], ffn_conv_b[l], ffn_w_down[l])
        x = _conv_ffn(x, wu, fcw, fcb, wd, _row(ln2_g[l]), _row(ln2_b[l]))
    return x.reshape(B, SEQ, D_MODEL)


def kernel(x_prompt, x_sample, w_in_even, rpb, sgu_ln_g, sgu_ln_b, sgu_w, sgu_b, w_out_even, w_in_odd, q_norm_g, w_uq, kv_norm_g, w_ukv, conv_w, conv_b, conv_ln_g, conv_ln_b, w_out_odd, ffn_w_up, ffn_conv_w, ffn_conv_b, ffn_w_down, ln1_g, ln1_b, ln2_g, ln2_b):
    n_prompt = x_prompt.shape[0]
    x = jnp.concatenate([x_prompt, x_sample], axis=0)
    y = _trunk(x, w_in_even, rpb, sgu_ln_g, sgu_ln_b, sgu_w, sgu_b, w_out_even,
               w_in_odd, q_norm_g, w_uq, kv_norm_g, w_ukv, conv_w, conv_b, conv_ln_g, conv_ln_b,
               w_out_odd, ffn_w_up, ffn_conv_w, ffn_conv_b, ffn_w_down, ln1_g, ln1_b, ln2_g, ln2_b)
    return (y[:n_prompt], y[n_prompt:])
```

```python
import functools

import numpy as np
import jax
import jax.numpy as jnp
from jax import lax
from jax.experimental import pallas as pl
from jax.experimental.pallas import tpu as pltpu

F32 = jnp.float32
BF16 = jnp.bfloat16

D_MODEL = 1024
SEQ = 2048
DEPTH = 4
GRID_W = 64
ROWS = SEQ // GRID_W
NA_HEADS = 8
NA_HEAD_DIM = 64
NA_WIDTH = 512
NA_WIN_R = 8
NA_WIN_C = 16
SGU_WIDTH = 512
SGU_GROUPS = 4
SGU_CHUNK = 128
MLA_HEADS = 8
MLA_Q_LORA = 256
MLA_KV_LORA = 128
MLA_NOPE = 64
MLA_ROPE = 32
MLA_V = 64
MLA_WIDTH = 512
ROPE_THETA = 10000.0
CONV_WIDTH = 512
CONV_K = 31
D_FF = 2816
LN_EPS = 1e-5
RMS_EPS = 1e-6
ALPHA = (2 * DEPTH) ** 0.25

VMEM_LIMIT_BYTES = 56 * 1024 * 1024
LANES = 128
HALO = 16
TM = 512
TILES_PER_SEQ = SEQ // TM
TM_FFN = 1024
FF_CHUNK = 256
N_FF_CHUNKS = D_FF // FF_CHUNK
NA_QROWS = 4
NA_KROWS = 12
NA_QBLK = NA_QROWS * GRID_W
NA_KBLK = NA_KROWS * GRID_W
NA_BLOCKS = ROWS // NA_QROWS
MLA_QBLK = 256
MLA_KTILE = 256
NA_KTILE = 256
LOG2E = 1.4426950408889634
NEG = -1e30


def _params(n_axes=1):
    return pltpu.CompilerParams(dimension_semantics=("arbitrary",) * n_axes,
                                vmem_limit_bytes=VMEM_LIMIT_BYTES)


def _const_spec(shape):
    nd = len(shape)
    return pl.BlockSpec(shape, lambda *_: (0,) * nd, pipeline_mode=pl.Buffered(1))


def _layernorm(z, g, b):
    mu = jnp.mean(z, axis=-1, keepdims=True)
    zc = z - mu
    var = jnp.mean(zc * zc, axis=-1, keepdims=True)
    return zc * lax.rsqrt(var + LN_EPS) * g + b


def _rmsnorm(z, g):
    return z * lax.rsqrt(jnp.mean(z * z, axis=-1, keepdims=True) + RMS_EPS) * g


def _sigmoid(z):
    return 1.0 / (1.0 + jnp.exp(-z))


def _dot(a, b):
    return jnp.dot(a, b, preferred_element_type=F32)


def _dot_nt(a, b):
    return lax.dot_general(a, b, (((1,), (1,)), ((), ())), preferred_element_type=F32)


def _even_in_kernel(x_ref, w_ref, g_ref, b_ref, qkv_ref, u_ref, v_ref):
    xb = x_ref[...].astype(BF16)
    qkv_ref[...] = _dot(xb, w_ref[:, :3 * NA_WIDTH]).astype(BF16)
    ug = _dot(xb, w_ref[:, 3 * NA_WIDTH:])
    u_ref[...] = jax.nn.gelu(ug[:, :SGU_WIDTH])
    v = _layernorm(jax.nn.gelu(ug[:, SGU_WIDTH:]), g_ref[...], b_ref[...])
    v_ref[...] = v.astype(BF16)


def _even_in(x, w, ln_g, ln_b):
    T = x.shape[0]
    tok = lambda n: pl.BlockSpec((TM, n), lambda i: (i, 0))
    return pl.pallas_call(
        _even_in_kernel,
        grid=(T // TM,),
        in_specs=[tok(D_MODEL), _const_spec(w.shape), _const_spec(ln_g.shape), _const_spec(ln_b.shape)],
        out_specs=[tok(3 * NA_WIDTH), tok(SGU_WIDTH), tok(SGU_WIDTH)],
        out_shape=[jax.ShapeDtypeStruct((T, 3 * NA_WIDTH), BF16),
                   jax.ShapeDtypeStruct((T, SGU_WIDTH), F32),
                   jax.ShapeDtypeStruct((T, SGU_WIDTH), BF16)],
        compiler_params=_params(),
        name="even_in",
    )(x, w, ln_g, ln_b)


def _rows(i, size, n=None):
    n = size if n is None else n
    start = i * size
    if not isinstance(i, int):
        start = pl.multiple_of(start, size)
    return pl.ds(start, n)


def _clip(v, lo, hi):
    return min(max(v, lo), hi) if isinstance(v, int) else jnp.clip(v, lo, hi)


def _pair_attention(n_blocks, n_tiles, qb, kt, q_heads, k_tile, v_tile, bias_tile, write, s_bufs, m_bufs):
    lane_chunks = kt // LANES

    def run(prod, cons):
        if prod is not None:
            ip, s_p, m_p = prod
            qh = q_heads(ip)
            mx = [jnp.full((qb, LANES), NEG, F32) for _ in range(2)]
        if cons is not None:
            ic, s_c, m_c = cons
            m = [m_c[h] for h in range(2)]
            acc = [jnp.zeros((qb, LANES), F32) for _ in range(2)]
        for t in range(n_tiles):
            if prod is not None:
                k = k_tile(ip, t)
                for h in range(2):
                    s = _dot_nt(qh[h], k)
                    b = bias_tile(ip, t, h)
                    if b is not None:
                        s = s + b
                    s_p[h, :, t * kt:(t + 1) * kt] = s
                    for c in range(lane_chunks):
                        mx[h] = jnp.maximum(mx[h], s[:, c * LANES:(c + 1) * LANES])
            if cons is not None:
                for h in range(2):
                    p = [jnp.exp2(s_c[h, :, t * kt + c * LANES:t * kt + (c + 1) * LANES] - m[h])
                         for c in range(lane_chunks)]
                    p = jnp.concatenate(p, axis=1).astype(BF16)
                    acc[h] = acc[h] + _dot(p, v_tile(ic, t, h))
        if prod is not None:
            for h in range(2):
                m_p[h] = jnp.broadcast_to(jnp.max(mx[h], axis=-1, keepdims=True), (qb, LANES))
        if cons is not None:
            lane = lax.broadcasted_iota(jnp.int32, (1, LANES), 1)
            halves = [acc[h] * pltpu.roll(1.0 / acc[h], LANES // 2, 1) for h in range(2)]
            write(ic, jnp.where(lane < LANES // 2, halves[0], halves[1]))

    assert n_blocks % 2 == 0 and n_blocks >= 2
    buf_a = (s_bufs[0], m_bufs[0])
    buf_b = (s_bufs[1], m_bufs[1])
    run((0,) + buf_a, None)

    def pair(u, carry):
        i = 2 * u
        run((i + 1,) + buf_b, (i,) + buf_a)
        run((i + 2,) + buf_a, (i + 1,) + buf_b)
        return carry

    lax.fori_loop(0, n_blocks // 2 - 1, pair, 0)
    run((n_blocks - 1,) + buf_b, (n_blocks - 2,) + buf_a)
    run(None, (n_blocks - 1,) + buf_b)


def _nbr_attn_kernel(q_ref, k_ref, v_ref, bias_ref, o_ref, s_a, s_b, m_a, m_b):
    lane = lax.broadcasted_iota(jnp.int32, (1, LANES), 1)
    head_lanes = (lane < NA_HEAD_DIM, lane >= NA_HEAD_DIM)

    def key_rows(m, t):
        key_row0 = _clip(NA_QROWS * m - NA_WIN_R // 2, 0, ROWS - NA_KROWS)
        return _rows(key_row0 + t * (NA_KTILE // GRID_W), GRID_W, NA_KTILE)

    def q_heads(m):
        q = q_ref[0, _rows(m, NA_QBLK), :]
        return [jnp.where(head_lanes[h], q, jnp.zeros_like(q)) for h in range(2)]

    def k_tile(m, t):
        return k_ref[0, key_rows(m, t), :]

    def v_tile(m, t, h):
        v = v_ref[0, key_rows(m, t), :]
        return jnp.where(head_lanes[h], v, jnp.ones_like(v))

    def bias_tile(m, t, h):
        if isinstance(m, int):
            cls = 0 if m == 0 else (2 if m == NA_BLOCKS - 1 else 1)
        else:
            cls = jnp.where(m == 0, 0, jnp.where(m == NA_BLOCKS - 1, 2, 1))
        return bias_ref[cls, h, :, t * NA_KTILE:(t + 1) * NA_KTILE]

    def write(m, out):
        o_ref[0, _rows(m, NA_QBLK), :] = out.astype(BF16)

    _pair_attention(NA_BLOCKS, NA_KBLK // NA_KTILE, NA_QBLK, NA_KTILE, q_heads, k_tile, v_tile, bias_tile,
                    write, (s_a, s_b), (m_a, m_b))


def _nbr_attn(qkv, bias):
    B = qkv.shape[0]
    pairs = NA_HEADS // 2
    col = lambda off: pl.BlockSpec((1, SEQ, LANES), lambda p, b: (b, 0, off + p))
    return pl.pallas_call(
        _nbr_attn_kernel,
        grid=(pairs, B),
        in_specs=[col(0), col(pairs), col(2 * pairs),
                  pl.BlockSpec((3, 2, NA_QBLK, NA_KBLK), lambda p, b: (0, p, 0, 0))],
        out_specs=col(0),
        out_shape=jax.ShapeDtypeStruct((B, SEQ, NA_WIDTH), BF16),
        scratch_shapes=[pltpu.VMEM((2, NA_QBLK, NA_KBLK), F32)] * 2 + [pltpu.VMEM((2, NA_QBLK, LANES), F32)] * 2,
        compiler_params=_params(2),
        name="nbr_attn",
    )(qkv, qkv, qkv, bias)


def _out_proj_ln(x, mixer_lo, mixer_hi, w_ref, g_ref, b_ref):
    half = w_ref.shape[0] // 2
    m = _dot(mixer_lo, w_ref[:half, :]) + _dot(mixer_hi, w_ref[half:, :])
    return _layernorm(ALPHA * x + m, g_ref[...], b_ref[...])


def _even_out_kernel(a_ref, u_ref, v_ref, x_ref, sw_ref, sb_ref, w_ref, g_ref, b_ref, o_ref, gate_scr):
    for c in range(TM // SGU_CHUNK):
        rows = slice(c * SGU_CHUNK, (c + 1) * SGU_CHUNK)
        for g in range(SGU_GROUPS):
            cols = slice(g * LANES, (g + 1) * LANES)
            mixed = _dot(sw_ref[g], v_ref[rows, cols]) + sb_ref[g]
            gate_scr[rows, cols] = (u_ref[rows, cols] * mixed).astype(BF16)
    o_ref[...] = _out_proj_ln(x_ref[...], a_ref[...], gate_scr[...], w_ref, g_ref, b_ref)


def _even_out(a, u, v, x, sgu_w, sgu_bias, w_out, ln_g, ln_b):
    T = x.shape[0]
    tok = lambda n: pl.BlockSpec((TM, n), lambda i: (i, 0))
    consts = (sgu_w, sgu_bias, w_out, ln_g, ln_b)
    return pl.pallas_call(
        _even_out_kernel,
        grid=(T // TM,),
        in_specs=[tok(NA_WIDTH), tok(SGU_WIDTH), tok(SGU_WIDTH), tok(D_MODEL)]
                 + [_const_spec(c.shape) for c in consts],
        out_specs=tok(D_MODEL),
        out_shape=jax.ShapeDtypeStruct((T, D_MODEL), F32),
        scratch_shapes=[pltpu.VMEM((TM, SGU_WIDTH), BF16)],
        compiler_params=_params(),
        name="even_out",
    )(a, u, v, x, *consts)


def _odd_in_kernel(x_ref, w_ref, qg_ref, wq_ref, kg_ref, wkv_ref, csq_ref, csk_ref,
                   q_ref, kn_ref, kr_ref, v_ref, d_ref):
    xb = x_ref[...].astype(BF16)
    h = _dot(xb, w_ref[...])
    o1 = MLA_Q_LORA
    o2 = o1 + MLA_KV_LORA
    o3 = o2 + LANES
    o4 = o3 + CONV_WIDTH

    cq = _rmsnorm(h[:, :o1], qg_ref[...]).astype(BF16)
    qf = _dot(cq, wq_ref[...])
    csq = csq_ref[...]
    for p in range(MLA_HEADS // 2):
        base = 2 * LANES * p
        q_ref[:, base:base + LANES] = qf[:, base:base + LANES].astype(BF16)
        prod = qf[:, base + LANES:base + 2 * LANES] * csq
        q_ref[:, base + LANES:base + 2 * LANES] = (prod + pltpu.roll(prod, 64, 1)).astype(BF16)

    ckv = _rmsnorm(h[:, o1:o2], kg_ref[...]).astype(BF16)
    kv = _dot(ckv, wkv_ref[...])
    kn_ref[...] = kv[:, :MLA_WIDTH].astype(BF16)
    v_ref[...] = kv[:, MLA_WIDTH:].astype(BF16)

    prod = h[:, o2:o3] * csk_ref[...]
    roped = prod + pltpu.roll(prod, LANES - MLA_ROPE, 1)
    lane = lax.broadcasted_iota(jnp.int32, (1, LANES), 1)
    kr_ref[...] = jnp.where(lane < MLA_ROPE, roped, pltpu.roll(roped, MLA_ROPE, 1)).astype(BF16)

    d_ref[...] = h[:, o3:o4] * _sigmoid(h[:, o4:])


def _odd_in(x, w, q_g, wq, kv_g, wkv, csq, csk):
    T = x.shape[0]
    tok = lambda n: pl.BlockSpec((TM, n), lambda i: (i, 0))
    pos = pl.BlockSpec((TM, LANES), lambda i: (i % TILES_PER_SEQ, 0))
    return pl.pallas_call(
        _odd_in_kernel,
        grid=(T // TM,),
        in_specs=[tok(D_MODEL), _const_spec(w.shape), _const_spec(q_g.shape), _const_spec(wq.shape),
                  _const_spec(kv_g.shape), _const_spec(wkv.shape), pos, pos],
        out_specs=[tok(2 * MLA_WIDTH), tok(MLA_WIDTH), tok(LANES), tok(MLA_WIDTH), tok(CONV_WIDTH)],
        out_shape=[jax.ShapeDtypeStruct((T, 2 * MLA_WIDTH), BF16),
                   jax.ShapeDtypeStruct((T, MLA_WIDTH), BF16),
                   jax.ShapeDtypeStruct((T, LANES), BF16),
                   jax.ShapeDtypeStruct((T, MLA_WIDTH), BF16),
                   jax.ShapeDtypeStruct((T, CONV_WIDTH), F32)],
        compiler_params=_params(),
        name="odd_in",
    )(x, w, q_g, wq, kv_g, wkv, csq, csk)


def _latent_attn_kernel(q_ref, kn_ref, kr_ref, v_ref, o_ref, k_scr, v_scr, s_a, s_b, m_a, m_b):
    lane = lax.broadcasted_iota(jnp.int32, (1, LANES), 1)
    lane2 = lax.broadcasted_iota(jnp.int32, (1, 2 * LANES), 1)
    k_scr[:, :LANES] = kn_ref[0]
    k_scr[:, LANES:] = kr_ref[0]
    v = v_ref[0]
    v_lanes = (lane < MLA_V, lane >= MLA_V)
    q_lanes = []
    for h in range(2):
        v_scr[h] = jnp.where(v_lanes[h], v, jnp.ones_like(v))
        nope = (lane2 >= h * MLA_NOPE) & (lane2 < (h + 1) * MLA_NOPE)
        rope = (lane2 >= LANES + h * MLA_ROPE) & (lane2 < LANES + (h + 1) * MLA_ROPE)
        q_lanes.append(nope | rope)

    def q_heads(i):
        q = q_ref[0, _rows(i, MLA_QBLK), :]
        return [jnp.where(q_lanes[h], q, jnp.zeros_like(q)) for h in range(2)]

    def k_tile(i, t):
        return k_scr[t * MLA_KTILE:(t + 1) * MLA_KTILE, :]

    def v_tile(i, t, h):
        return v_scr[h, t * MLA_KTILE:(t + 1) * MLA_KTILE, :]

    def write(i, out):
        o_ref[0, _rows(i, MLA_QBLK), :] = out.astype(BF16)

    _pair_attention(SEQ // MLA_QBLK, SEQ // MLA_KTILE, MLA_QBLK, MLA_KTILE, q_heads, k_tile, v_tile,
                    lambda i, t, h: None, write, (s_a, s_b), (m_a, m_b))


def _latent_attn(q, kn, kr, v):
    B = q.shape[0]
    pairs = MLA_HEADS // 2
    col = lambda width: pl.BlockSpec((1, SEQ, width), lambda b, p: (b, 0, p))
    return pl.pallas_call(
        _latent_attn_kernel,
        grid=(B, pairs),
        in_specs=[col(2 * LANES), col(LANES), pl.BlockSpec((1, SEQ, LANES), lambda b, p: (b, 0, 0)), col(LANES)],
        out_specs=col(LANES),
        out_shape=jax.ShapeDtypeStruct((B, SEQ, MLA_WIDTH), BF16),
        scratch_shapes=[pltpu.VMEM((SEQ, 2 * LANES), BF16), pltpu.VMEM((2, SEQ, LANES), BF16)]
                       + [pltpu.VMEM((2, MLA_QBLK, SEQ), F32)] * 2 + [pltpu.VMEM((2, MLA_QBLK, LANES), F32)] * 2,
        compiler_params=_params(2),
        name="latent_attn",
    )(q, kn, kr, v)


def _fill_halo_buffer(buf, prev_ref, cur_ref, next_ref, dtype):
    tm = cur_ref.shape[0]
    tiles_per_seq = SEQ // tm
    i = pl.program_id(0) % tiles_per_seq
    prev = prev_ref[...]
    nxt = next_ref[...]
    buf[:HALO, :] = jnp.where(i == 0, jnp.zeros_like(prev), prev).astype(dtype)
    buf[HALO:HALO + tm, :] = cur_ref[...].astype(dtype)
    buf[HALO + tm:, :] = jnp.where(i == tiles_per_seq - 1, jnp.zeros_like(nxt), nxt).astype(dtype)


def _halo_specs(T, tm, n):
    per = tm // HALO
    last = T // HALO - 1
    return (pl.BlockSpec((HALO, n), lambda i: (jnp.maximum(i * per - 1, 0), 0)),
            pl.BlockSpec((HALO, n), lambda i: (jnp.minimum((i + 1) * per, last), 0)))


def _odd_out_kernel(c_ref, dp_ref, d_ref, dn_ref, x_ref, cw_ref, cb_ref, cg_ref, cbeta_ref,
                    w_ref, g_ref, b_ref, o_ref, buf, conv_scr):
    _fill_halo_buffer(buf, dp_ref, d_ref, dn_ref, F32)
    pad = CONV_K // 2
    for c in range(CONV_WIDTH // LANES):
        cols = slice(c * LANES, (c + 1) * LANES)
        acc = jnp.zeros((TM, LANES), F32) + cb_ref[:, cols]
        for k in range(CONV_K):
            acc = acc + buf[HALO - pad + k:HALO - pad + k + TM, cols] * cw_ref[k:k + 1, cols]
        conv_scr[:, cols] = acc
    y = _layernorm(conv_scr[...], cg_ref[...], cbeta_ref[...])
    y = (y * _sigmoid(y)).astype(BF16)
    o_ref[...] = _out_proj_ln(x_ref[...], c_ref[...], y, w_ref, g_ref, b_ref)


def _odd_out(c, d, x, conv_w, conv_b, conv_g, conv_beta, w_out, ln_g, ln_b):
    T = x.shape[0]
    tok = lambda n: pl.BlockSpec((TM, n), lambda i: (i, 0))
    prev, nxt = _halo_specs(T, TM, CONV_WIDTH)
    consts = (conv_w, conv_b, conv_g, conv_beta, w_out, ln_g, ln_b)
    return pl.pallas_call(
        _odd_out_kernel,
        grid=(T // TM,),
        in_specs=[tok(MLA_WIDTH), prev, tok(CONV_WIDTH), nxt, tok(D_MODEL)]
                 + [_const_spec(a.shape) for a in consts],
        out_specs=tok(D_MODEL),
        out_shape=jax.ShapeDtypeStruct((T, D_MODEL), F32),
        scratch_shapes=[pltpu.VMEM((TM + 2 * HALO, CONV_WIDTH), F32), pltpu.VMEM((TM, CONV_WIDTH), F32)],
        compiler_params=_params(),
        name="odd_out",
    )(c, d, d, d, x, *consts)


def _conv_ffn_kernel(xp_ref, x_ref, xn_ref, wu_ref, cw_ref, cb_ref, wd_ref, g_ref, b_ref, o_ref,
                     lhs, h_a, h_b, acc):
    _fill_halo_buffer(lhs, xp_ref, x_ref, xn_ref, BF16)

    def up(j, h_scr):
        h_scr[...] = _dot(lhs[...], wu_ref[j])

    def down(j, h_scr, first=False):
        cw = cw_ref[j]
        conv = (h_scr[HALO - 1:HALO - 1 + TM_FFN, :] * cw[0:1, :]
                + h_scr[HALO:HALO + TM_FFN, :] * cw[1:2, :]
                + h_scr[HALO + 1:HALO + 1 + TM_FFN, :] * cw[2:3, :]
                + cb_ref[j])
        gate = conv[:, :FF_CHUNK]
        act = (gate * _sigmoid(gate) * conv[:, FF_CHUNK:]).astype(BF16)
        part = _dot(act, wd_ref[j])
        if first:
            acc[...] = part
        else:
            acc[...] += part

    up(0, h_a)
    up(1, h_b)
    down(0, h_a, first=True)

    def pair(i, carry):
        j = 2 * i + 1
        up(j + 1, h_a)
        down(j, h_b)
        up(j + 2, h_b)
        down(j + 1, h_a)
        return carry

    assert N_FF_CHUNKS % 2 == 1 and N_FF_CHUNKS >= 3
    lax.fori_loop(0, (N_FF_CHUNKS - 3) // 2, pair, 0)
    up(N_FF_CHUNKS - 1, h_a)
    down(N_FF_CHUNKS - 2, h_b)
    down(N_FF_CHUNKS - 1, h_a)
    o_ref[...] = _layernorm(ALPHA * x_ref[...] + acc[...], g_ref[...], b_ref[...])


def _conv_ffn(x, w_up, conv_w, conv_b, w_down, ln_g, ln_b):
    T = x.shape[0]
    tok = pl.BlockSpec((TM_FFN, D_MODEL), lambda i: (i, 0))
    prev, nxt = _halo_specs(T, TM_FFN, D_MODEL)
    consts = (w_up, conv_w, conv_b, w_down, ln_g, ln_b)
    h_buf = pltpu.VMEM((TM_FFN + 2 * HALO, 2 * FF_CHUNK), F32)
    return pl.pallas_call(
        _conv_ffn_kernel,
        grid=(T // TM_FFN,),
        in_specs=[prev, tok, nxt] + [_const_spec(a.shape) for a in consts],
        out_specs=tok,
        out_shape=jax.ShapeDtypeStruct((T, D_MODEL), F32),
        scratch_shapes=[pltpu.VMEM((TM_FFN + 2 * HALO, D_MODEL), BF16), h_buf, h_buf,
                        pltpu.VMEM((TM_FFN, D_MODEL), F32)],
        compiler_params=_params(),
        name="conv_ffn",
    )(x, x, x, *consts)


def _row(v):
    return v.reshape(1, -1).astype(F32)


def _nbr_bias(rpb):
    c = np.arange(GRID_W)
    c0 = np.clip(c - NA_WIN_C // 2, 0, GRID_W - NA_WIN_C)
    col_ok = (c[None, :] >= c0[:, None]) & (c[None, :] < c0[:, None] + NA_WIN_C)
    dc = np.clip(c[None, :] - c[:, None], -(NA_WIN_C - 1), NA_WIN_C - 1) + NA_WIN_C - 1
    row_tab = jnp.where(col_ok[None, None], rpb[:, :, dc] * LOG2E, NEG)
    row_tab = jnp.concatenate([row_tab, jnp.full_like(row_tab[:, :1], NEG)], axis=1)
    masked = 2 * NA_WIN_R - 1
    sel = np.full((3, NA_QROWS, NA_KROWS), masked)
    for cls, m in enumerate((0, 1, NA_BLOCKS - 1)):
        key_row0 = min(max(NA_QROWS * m - NA_WIN_R // 2, 0), ROWS - NA_KROWS)
        for i in range(NA_QROWS):
            qr = NA_QROWS * m + i
            r0 = min(max(qr - NA_WIN_R // 2, 0), ROWS - NA_WIN_R)
            for j in range(NA_KROWS):
                kr = key_row0 + j
                if r0 <= kr < r0 + NA_WIN_R:
                    sel[cls, i, j] = kr - qr + NA_WIN_R - 1
    b = row_tab[:, sel]
    b = b.transpose(1, 0, 2, 4, 3, 5)
    return b.reshape(3, NA_HEADS, NA_QBLK, NA_KBLK).astype(F32)


def _rope_tables():
    pos = jnp.arange(SEQ, dtype=F32)
    inv = ROPE_THETA ** (-jnp.arange(0, MLA_ROPE, 2, dtype=F32) / MLA_ROPE)
    ang = pos[:, None] * inv[None, :]
    cos, sin = jnp.cos(ang), jnp.sin(ang)
    csq = jnp.concatenate([jnp.tile(cos, (1, 4)), jnp.tile(sin, (1, 4))], axis=1)
    csk = jnp.concatenate([jnp.tile(cos, (1, 2)), jnp.tile(sin, (1, 2)),
                           jnp.zeros((SEQ, LANES - 2 * MLA_ROPE), F32)], axis=1)
    return csq, csk


def _rot_cols(w):
    half = w.shape[-1] // 2
    return jnp.concatenate([-w[..., half:], w[..., :half]], axis=-1)


def _odd_weights(w_in, w_uq, w_ukv):
    o1 = MLA_Q_LORA
    o2 = o1 + MLA_KV_LORA
    o3 = o2 + MLA_ROPE
    k_r = w_in[:, o2:o3]
    k_blk = jnp.concatenate([k_r, _rot_cols(k_r), jnp.zeros((D_MODEL, LANES - 2 * MLA_ROPE), F32)], axis=1)
    w_in2 = jnp.concatenate([w_in[:, :o2], k_blk, w_in[:, o3:]], axis=1)

    scale = (MLA_NOPE + MLA_ROPE) ** -0.5 * LOG2E
    wq =(w_uq * scale).reshape(MLA_Q_LORA, MLA_HEADS, MLA_NOPE + MLA_ROPE)
    nope, rope = wq[:, :, :MLA_NOPE], wq[:, :, MLA_NOPE:]
    rot = _rot_cols(rope)
    blocks = []
    for p in range(MLA_HEADS // 2):
        h0, h1 = 2 * p, 2 * p + 1
        blocks += [nope[:, h0], nope[:, h1], rope[:, h0], rope[:, h1], rot[:, h0], rot[:, h1]]
    wq2 = jnp.concatenate(blocks, axis=1)

    wkv = w_ukv.reshape(MLA_KV_LORA, MLA_HEADS, MLA_NOPE + MLA_V)
    wkv2 = jnp.concatenate([wkv[:, :, :MLA_NOPE].reshape(MLA_KV_LORA, -1),
                            wkv[:, :, MLA_NOPE:].reshape(MLA_KV_LORA, -1)], axis=1)
    return w_in2.astype(BF16), wq2.astype(BF16), wkv2.astype(BF16)


def _ffn_weights(w_up, conv_w, conv_b, w_down):
    def chunked(a):
        g = a[..., :D_FF].reshape(a.shape[:-1] + (N_FF_CHUNKS, FF_CHUNK))
        v = a[..., D_FF:].reshape(a.shape[:-1] + (N_FF_CHUNKS, FF_CHUNK))
        return jnp.moveaxis(jnp.concatenate([g, v], axis=-1), -2, 0)
    return (chunked(w_up).astype(BF16), chunked(conv_w).astype(F32), chunked(conv_b[None, :]).astype(F32),
            w_down.reshape(N_FF_CHUNKS, FF_CHUNK, D_MODEL).astype(BF16))


def _trunk(x, w_in_even, rpb, sgu_ln_g, sgu_ln_b, sgu_w, sgu_b, w_out_even,
           w_in_odd, q_norm_g, w_uq, kv_norm_g, w_ukv, conv_w, conv_b, conv_ln_g, conv_ln_b,
           w_out_odd, ffn_w_up, ffn_conv_w, ffn_conv_b, ffn_w_down, ln1_g, ln1_b, ln2_g, ln2_b):
    B = x.shape[0]
    T = B * SEQ
    x = x.reshape(T, D_MODEL)
    csq, csk = _rope_tables()
    for l in range(DEPTH):
        i = l // 2
        if l % 2 == 0:
            w_in = jnp.concatenate([w_in_even[i][:, :NA_WIDTH] * (NA_HEAD_DIM ** -0.5 * LOG2E),
                                    w_in_even[i][:, NA_WIDTH:]], axis=1).astype(BF16)
            qkv, u, v = _even_in(x, w_in, _row(sgu_ln_g[i]), _row(sgu_ln_b[i]))
            a = _nbr_attn(qkv.reshape(B, SEQ, 3 * NA_WIDTH), _nbr_bias(rpb[i])).reshape(T, NA_WIDTH)
            sgu_bias = jnp.broadcast_to(sgu_b[i][:, :, None], (SGU_GROUPS, SGU_CHUNK, LANES)).astype(F32)
            x = _even_out(a, u, v, x, sgu_w[i].astype(BF16), sgu_bias, w_out_even[i].astype(BF16),
                          _row(ln1_g[l]), _row(ln1_b[l]))
        else:
            w_in, wq, wkv = _odd_weights(w_in_odd[i], w_uq[i], w_ukv[i])
            q, kn, kr, v, d = _odd_in(x, w_in, _row(q_norm_g[i]), wq, _row(kv_norm_g[i]), wkv, csq, csk)
            c = _latent_attn(q.reshape(B, SEQ, -1), kn.reshape(B, SEQ, -1), kr.reshape(B, SEQ, -1),
                             v.reshape(B, SEQ, -1)).reshape(T, MLA_WIDTH)
            cw = jnp.concatenate([conv_w[i], jnp.zeros((1, CONV_WIDTH), F32)], axis=0)
            x = _odd_out(c, d, x, cw, _row(conv_b[i]), _row(conv_ln_g[i]), _row(conv_ln_b[i]),
                         w_out_odd[i].astype(BF16), _row(ln1_g[l]), _row(ln1_b[l]))
        wu, fcw, fcb, wd = _ffn_weights(ffn_w_up[l], ffn_conv_w[l], ffn_conv_b[l], ffn_w_down[l])
        x = _conv_ffn(x, wu, fcw, fcb, wd, _row(ln2_g[l]), _row(ln2_b[l]))
    return x.reshape(B, SEQ, D_MODEL)


def kernel(x_prompt, x_sample, w_in_even, rpb, sgu_ln_g, sgu_ln_b, sgu_w, sgu_b, w_out_even, w_in_odd, q_norm_g, w_uq, kv_norm_g, w_ukv, conv_w, conv_b, conv_ln_g, conv_ln_b, w_out_odd, ffn_w_up, ffn_conv_w, ffn_conv_b, ffn_w_down, ln1_g, ln1_b, ln2_g, ln2_b):
    n_prompt = x_prompt.shape[0]
    x = jnp.concatenate([x_prompt, x_sample], axis=0)
    y = _trunk(x, w_in_even, rpb, sgu_ln_g, sgu_ln_b, sgu_w, sgu_b, w_out_even,
               w_in_odd, q_norm_g, w_uq, kv_norm_g, w_ukv, conv_w, conv_b, conv_ln_g, conv_ln_b,
               w_out_odd, ffn_w_up, ffn_conv_w, ffn_conv_b, ffn_w_down, ln1_g, ln1_b, ln2_g, ln2_b)
    return (y[:n_prompt], y[n_prompt:])
```

```python
import functools

import numpy as np
import jax
import jax.numpy as jnp
from jax import lax
from jax.experimental import pallas as pl
from jax.experimental.pallas import tpu as pltpu

F32 = jnp.float32
BF16 = jnp.bfloat16

D_MODEL = 1024
SEQ = 2048
DEPTH = 4
GRID_W = 64
ROWS = SEQ // GRID_W
NA_HEADS = 8
NA_HEAD_DIM = 64
NA_WIDTH = 512
NA_WIN_R = 8
NA_WIN_C = 16
SGU_WIDTH = 512
SGU_GROUPS = 4
SGU_CHUNK = 128
MLA_HEADS = 8
MLA_Q_LORA = 256
MLA_KV_LORA = 128
MLA_NOPE = 64
MLA_ROPE = 32
MLA_V = 64
MLA_WIDTH = 512
ROPE_THETA = 10000.0
CONV_WIDTH = 512
CONV_K = 31
D_FF = 2816
LN_EPS = 1e-5
RMS_EPS = 1e-6
ALPHA = (2 * DEPTH) ** 0.25

VMEM_LIMIT_BYTES = 56 * 1024 * 1024
LANES = 128
HALO = 16
TM = 512
TILES_PER_SEQ = SEQ // TM
TM_FFN = 1024
CONV_ROWS = 128
FF_ROWS = 256
FF_CHUNK = 256
N_FF_CHUNKS = D_FF // FF_CHUNK
NA_QROWS = 4
NA_KROWS = 12
NA_QBLK = NA_QROWS * GRID_W
NA_KBLK = NA_KROWS * GRID_W
NA_BLOCKS = ROWS // NA_QROWS
MLA_QBLK = 256
MLA_KTILE = 256
NA_KTILE = 256
LOG2E = 1.4426950408889634
NEG = -1e30


def _params(n_axes=1):
    return pltpu.CompilerParams(dimension_semantics=("arbitrary",) * n_axes,
                                vmem_limit_bytes=VMEM_LIMIT_BYTES)


def _const_spec(shape):
    nd = len(shape)
    return pl.BlockSpec(shape, lambda *_: (0,) * nd, pipeline_mode=pl.Buffered(1))


def _layernorm(z, g, b):
    mu = jnp.mean(z, axis=-1, keepdims=True)
    zc = z - mu
    var = jnp.mean(zc * zc, axis=-1, keepdims=True)
    return zc * lax.rsqrt(var + LN_EPS) * g + b


def _rmsnorm(z, g):
    return z * lax.rsqrt(jnp.mean(z * z, axis=-1, keepdims=True) + RMS_EPS) * g


def _sigmoid(z):
    return 1.0 / (1.0 + jnp.exp(-z))


def _dot(a, b):
    return jnp.dot(a, b, preferred_element_type=F32)


def _dot_nt(a, b):
    return lax.dot_general(a, b, (((1,), (1,)), ((), ())), preferred_element_type=F32)


def _even_in_kernel(x_ref, w_ref, g_ref, b_ref, qkv_ref, u_ref, v_ref):
    xb = x_ref[...].astype(BF16)
    qkv_ref[...] = _dot(xb, w_ref[:, :3 * NA_WIDTH]).astype(BF16)
    ug = _dot(xb, w_ref[:, 3 * NA_WIDTH:])
    u_ref[...] = jax.nn.gelu(ug[:, :SGU_WIDTH])
    v = _layernorm(jax.nn.gelu(ug[:, SGU_WIDTH:]), g_ref[...], b_ref[...])
    v_ref[...] = v.astype(BF16)


def _even_in(x, w, ln_g, ln_b):
    T = x.shape[0]
    tok = lambda n: pl.BlockSpec((TM, n), lambda i: (i, 0))
    return pl.pallas_call(
        _even_in_kernel,
        grid=(T // TM,),
        in_specs=[tok(D_MODEL), _const_spec(w.shape), _const_spec(ln_g.shape), _const_spec(ln_b.shape)],
        out_specs=[tok(3 * NA_WIDTH), tok(SGU_WIDTH), tok(SGU_WIDTH)],
        out_shape=[jax.ShapeDtypeStruct((T, 3 * NA_WIDTH), BF16),
                   jax.ShapeDtypeStruct((T, SGU_WIDTH), F32),
                   jax.ShapeDtypeStruct((T, SGU_WIDTH), BF16)],
        compiler_params=_params(),
        name="even_in",
    )(x, w, ln_g, ln_b)


def _rows(i, size, n=None):
    n = size if n is None else n
    start = i * size
    if not isinstance(i, int):
        start = pl.multiple_of(start, size)
    return pl.ds(start, n)


def _clip(v, lo, hi):
    return min(max(v, lo), hi) if isinstance(v, int) else jnp.clip(v, lo, hi)


def _pair_attention(n_blocks, n_tiles, qb, kt, q_heads, k_tile, v_tile, bias_tile, write, s_bufs, m_bufs):
    lane_chunks = kt // LANES

    def run(prod, cons):
        if prod is not None:
            ip, s_p, m_p = prod
            qh = q_heads(ip)
            mx = [jnp.full((qb, LANES), NEG, F32) for _ in range(2)]
        if cons is not None:
            ic, s_c, m_c = cons
            m = [m_c[h] for h in range(2)]
            acc = [jnp.zeros((qb, LANES), F32) for _ in range(2)]
        for t in range(n_tiles):
            if prod is not None:
                k = k_tile(ip, t)
                for h in range(2):
                    s = _dot_nt(qh[h], k)
                    b = bias_tile(ip, t, h)
                    if b is not None:
                        s = s + b
                    s_p[h, :, t * kt:(t + 1) * kt] = s
                    for c in range(lane_chunks):
                        mx[h] = jnp.maximum(mx[h], s[:, c * LANES:(c + 1) * LANES])
            if cons is not None:
                for h in range(2):
                    p = [jnp.exp2(s_c[h, :, t * kt + c * LANES:t * kt + (c + 1) * LANES] - m[h])
                         for c in range(lane_chunks)]
                    p = jnp.concatenate(p, axis=1).astype(BF16)
                    acc[h] = acc[h] + _dot(p, v_tile(ic, t, h))
        if prod is not None:
            for h in range(2):
                m_p[h] = jnp.broadcast_to(jnp.max(mx[h], axis=-1, keepdims=True), (qb, LANES))
        if cons is not None:
            lane = lax.broadcasted_iota(jnp.int32, (1, LANES), 1)
            halves = [acc[h] * pltpu.roll(1.0 / acc[h], LANES // 2, 1) for h in range(2)]
            write(ic, jnp.where(lane < LANES // 2, halves[0], halves[1]))

    assert n_blocks % 2 == 0 and n_blocks >= 2
    buf_a = (s_bufs[0], m_bufs[0])
    buf_b = (s_bufs[1], m_bufs[1])
    run((0,) + buf_a, None)

    def pair(u, carry):
        i = 2 * u
        run((i + 1,) + buf_b, (i,) + buf_a)
        run((i + 2,) + buf_a, (i + 1,) + buf_b)
        return carry

    lax.fori_loop(0, n_blocks // 2 - 1, pair, 0)
    run((n_blocks - 1,) + buf_b, (n_blocks - 2,) + buf_a)
    run(None, (n_blocks - 1,) + buf_b)


def _nbr_attn_kernel(q_ref, k_ref, v_ref, bias_ref, o_ref, s_a, s_b, m_a, m_b):
    lane = lax.broadcasted_iota(jnp.int32, (1, LANES), 1)
    head_lanes = (lane < NA_HEAD_DIM, lane >= NA_HEAD_DIM)

    def key_rows(m, t):
        key_row0 = _clip(NA_QROWS * m - NA_WIN_R // 2, 0, ROWS - NA_KROWS)
        return _rows(key_row0 + t * (NA_KTILE // GRID_W), GRID_W, NA_KTILE)

    def q_heads(m):
        q = q_ref[0, _rows(m, NA_QBLK), :]
        return [jnp.where(head_lanes[h], q, jnp.zeros_like(q)) for h in range(2)]

    def k_tile(m, t):
        return k_ref[0, key_rows(m, t), :]

    def v_tile(m, t, h):
        v = v_ref[0, key_rows(m, t), :]
        return jnp.where(head_lanes[h], v, jnp.ones_like(v))

    def bias_tile(m, t, h):
        if isinstance(m, int):
            cls = 0 if m == 0 else (2 if m == NA_BLOCKS - 1 else 1)
        else:
            cls = jnp.where(m == 0, 0, jnp.where(m == NA_BLOCKS - 1, 2, 1))
        return bias_ref[cls, h, :, t * NA_KTILE:(t + 1) * NA_KTILE]

    def write(m, out):
        o_ref[0, _rows(m, NA_QBLK), :] = out.astype(BF16)

    _pair_attention(NA_BLOCKS, NA_KBLK // NA_KTILE, NA_QBLK, NA_KTILE, q_heads, k_tile, v_tile, bias_tile,
                    write, (s_a, s_b), (m_a, m_b))


def _nbr_attn(qkv, bias):
    B = qkv.shape[0]
    pairs = NA_HEADS // 2
    col = lambda off: pl.BlockSpec((1, SEQ, LANES), lambda p, b: (b, 0, off + p))
    return pl.pallas_call(
        _nbr_attn_kernel,
        grid=(pairs, B),
        in_specs=[col(0), col(pairs), col(2 * pairs),
                  pl.BlockSpec((3, 2, NA_QBLK, NA_KBLK), lambda p, b: (0, p, 0, 0))],
        out_specs=col(0),
        out_shape=jax.ShapeDtypeStruct((B, SEQ, NA_WIDTH), BF16),
        scratch_shapes=[pltpu.VMEM((2, NA_QBLK, NA_KBLK), F32)] * 2 + [pltpu.VMEM((2, NA_QBLK, LANES), F32)] * 2,
        compiler_params=_params(2),
        name="nbr_attn",
    )(qkv, qkv, qkv, bias)


def _out_proj_ln(x, mixer_lo, mixer_hi, w_ref, g_ref, b_ref):
    half = w_ref.shape[0] // 2
    m = _dot(mixer_lo, w_ref[:half, :]) + _dot(mixer_hi, w_ref[half:, :])
    return _layernorm(ALPHA * x + m, g_ref[...], b_ref[...])


def _even_out_kernel(a_ref, u_ref, v_ref, x_ref, sw_ref, sb_ref, w_ref, g_ref, b_ref, o_ref, gate_scr):
    for c in range(TM // SGU_CHUNK):
        rows = slice(c * SGU_CHUNK, (c + 1) * SGU_CHUNK)
        for g in range(SGU_GROUPS):
            cols = slice(g * LANES, (g + 1) * LANES)
            mixed = _dot(sw_ref[g], v_ref[rows, cols]) + sb_ref[g]
            gate_scr[rows, cols] = (u_ref[rows, cols] * mixed).astype(BF16)
    o_ref[...] = _out_proj_ln(x_ref[...], a_ref[...], gate_scr[...], w_ref, g_ref, b_ref)


def _even_out(a, u, v, x, sgu_w, sgu_bias, w_out, ln_g, ln_b):
    T = x.shape[0]
    tok = lambda n: pl.BlockSpec((TM, n), lambda i: (i, 0))
    consts = (sgu_w, sgu_bias, w_out, ln_g, ln_b)
    return pl.pallas_call(
        _even_out_kernel,
        grid=(T // TM,),
        in_specs=[tok(NA_WIDTH), tok(SGU_WIDTH), tok(SGU_WIDTH), tok(D_MODEL)]
                 + [_const_spec(c.shape) for c in consts],
        out_specs=tok(D_MODEL),
        out_shape=jax.ShapeDtypeStruct((T, D_MODEL), F32),
        scratch_shapes=[pltpu.VMEM((TM, SGU_WIDTH), BF16)],
        compiler_params=_params(),
        name="even_out",
    )(a, u, v, x, *consts)


def _odd_in_kernel(x_ref, w_ref, qg_ref, wq_ref, kg_ref, wkv_ref, csq_ref, csk_ref,
                   q_ref, kn_ref, kr_ref, v_ref, d_ref):
    xb = x_ref[...].astype(BF16)
    h = _dot(xb, w_ref[...])
    o1 = MLA_Q_LORA
    o2 = o1 + MLA_KV_LORA
    o3 = o2 + LANES
    o4 = o3 + CONV_WIDTH

    cq = _rmsnorm(h[:, :o1], qg_ref[...]).astype(BF16)
    qf = _dot(cq, wq_ref[...])
    csq = csq_ref[...]
    for p in range(MLA_HEADS // 2):
        base = 2 * LANES * p
        q_ref[:, base:base + LANES] = qf[:, base:base + LANES].astype(BF16)
        prod = qf[:, base + LANES:base + 2 * LANES] * csq
        q_ref[:, base + LANES:base + 2 * LANES] = (prod + pltpu.roll(prod, 64, 1)).astype(BF16)

    ckv = _rmsnorm(h[:, o1:o2], kg_ref[...]).astype(BF16)
    kv = _dot(ckv, wkv_ref[...])
    kn_ref[...] = kv[:, :MLA_WIDTH].astype(BF16)
    v_ref[...] = kv[:, MLA_WIDTH:].astype(BF16)

    prod = h[:, o2:o3] * csk_ref[...]
    roped = prod + pltpu.roll(prod, LANES - MLA_ROPE, 1)
    lane = lax.broadcasted_iota(jnp.int32, (1, LANES), 1)
    kr_ref[...] = jnp.where(lane < MLA_ROPE, roped, pltpu.roll(roped, MLA_ROPE, 1)).astype(BF16)

    d_ref[...] = h[:, o3:o4] * _sigmoid(h[:, o4:])


def _odd_in(x, w, q_g, wq, kv_g, wkv, csq, csk):
    T = x.shape[0]
    tok = lambda n: pl.BlockSpec((TM, n), lambda i: (i, 0))
    pos = pl.BlockSpec((TM, LANES), lambda i: (i % TILES_PER_SEQ, 0))
    return pl.pallas_call(
        _odd_in_kernel,
        grid=(T // TM,),
        in_specs=[tok(D_MODEL), _const_spec(w.shape), _const_spec(q_g.shape), _const_spec(wq.shape),
                  _const_spec(kv_g.shape), _const_spec(wkv.shape), pos, pos],
        out_specs=[tok(2 * MLA_WIDTH), tok(MLA_WIDTH), tok(LANES), tok(MLA_WIDTH), tok(CONV_WIDTH)],
        out_shape=[jax.ShapeDtypeStruct((T, 2 * MLA_WIDTH), BF16),
                   jax.ShapeDtypeStruct((T, MLA_WIDTH), BF16),
                   jax.ShapeDtypeStruct((T, LANES), BF16),
                   jax.ShapeDtypeStruct((T, MLA_WIDTH), BF16),
                   jax.ShapeDtypeStruct((T, CONV_WIDTH), F32)],
        compiler_params=_params(),
        name="odd_in",
    )(x, w, q_g, wq, kv_g, wkv, csq, csk)


def _latent_attn_kernel(q_ref, kn_ref, kr_ref, v_ref, o_ref, k_scr, v_scr, s_a, s_b, m_a, m_b):
    lane = lax.broadcasted_iota(jnp.int32, (1, LANES), 1)
    lane2 = lax.broadcasted_iota(jnp.int32, (1, 2 * LANES), 1)
    k_scr[:, :LANES] = kn_ref[0]
    k_scr[:, LANES:] = kr_ref[0]
    v = v_ref[0]
    v_lanes = (lane < MLA_V, lane >= MLA_V)
    q_lanes = []
    for h in range(2):
        v_scr[h] = jnp.where(v_lanes[h], v, jnp.ones_like(v))
        nope = (lane2 >= h * MLA_NOPE) & (lane2 < (h + 1) * MLA_NOPE)
        rope = (lane2 >= LANES + h * MLA_ROPE) & (lane2 < LANES + (h + 1) * MLA_ROPE)
        q_lanes.append(nope | rope)

    def q_heads(i):
        q = q_ref[0, _rows(i, MLA_QBLK), :]
        return [jnp.where(q_lanes[h], q, jnp.zeros_like(q)) for h in range(2)]

    def k_tile(i, t):
        return k_scr[t * MLA_KTILE:(t + 1) * MLA_KTILE, :]

    def v_tile(i, t, h):
        return v_scr[h, t * MLA_KTILE:(t + 1) * MLA_KTILE, :]

    def write(i, out):
        o_ref[0, _rows(i, MLA_QBLK), :] = out.astype(BF16)

    _pair_attention(SEQ // MLA_QBLK, SEQ // MLA_KTILE, MLA_QBLK, MLA_KTILE, q_heads, k_tile, v_tile,
                    lambda i, t, h: None, write, (s_a, s_b), (m_a, m_b))


def _latent_attn(q, kn, kr, v):
    B = q.shape[0]
    pairs = MLA_HEADS // 2
    col = lambda width: pl.BlockSpec((1, SEQ, width), lambda b, p: (b, 0, p))
    return pl.pallas_call(
        _latent_attn_kernel,
        grid=(B, pairs),
        in_specs=[col(2 * LANES), col(LANES), pl.BlockSpec((1, SEQ, LANES), lambda b, p: (b, 0, 0)), col(LANES)],
        out_specs=col(LANES),
        out_shape=jax.ShapeDtypeStruct((B, SEQ, MLA_WIDTH), BF16),
        scratch_shapes=[pltpu.VMEM((SEQ, 2 * LANES), BF16), pltpu.VMEM((2, SEQ, LANES), BF16)]
                       + [pltpu.VMEM((2, MLA_QBLK, SEQ), F32)] * 2 + [pltpu.VMEM((2, MLA_QBLK, LANES), F32)] * 2,
        compiler_params=_params(2),
        name="latent_attn",
    )(q, kn, kr, v)


def _fill_halo_buffer(buf, prev_ref, cur_ref, next_ref, dtype):
    tm = cur_ref.shape[0]
    tiles_per_seq = SEQ // tm
    i = pl.program_id(0) % tiles_per_seq
    prev = prev_ref[...]
    nxt = next_ref[...]
    buf[:HALO, :] = jnp.where(i == 0, jnp.zeros_like(prev), prev).astype(dtype)
    buf[HALO:HALO + tm, :] = cur_ref[...].astype(dtype)
    buf[HALO + tm:, :] = jnp.where(i == tiles_per_seq - 1, jnp.zeros_like(nxt), nxt).astype(dtype)


def _halo_specs(T, tm, n):
    per = tm // HALO
    last = T // HALO - 1
    return (pl.BlockSpec((HALO, n), lambda i: (jnp.maximum(i * per - 1, 0), 0)),
            pl.BlockSpec((HALO, n), lambda i: (jnp.minimum((i + 1) * per, last), 0)))


def _odd_out_kernel(c_ref, dp_ref, d_ref, dn_ref, x_ref, cw_ref, cb_ref, cg_ref, cbeta_ref,
                    w_ref, g_ref, b_ref, o_ref, buf, conv_scr, shift_scr):
    _fill_halo_buffer(buf, dp_ref, d_ref, dn_ref, F32)
    first = HALO - CONV_K // 2
    sub = 8
    reach = (first + CONV_K - 1) // sub * sub
    for phase in range(1, sub):
        shift_scr[phase - 1] = buf[phase:phase + TM + reach, :]
    for c in range(CONV_WIDTH // LANES):
        cols = slice(c * LANES, (c + 1) * LANES)
        for r0 in range(0, TM, CONV_ROWS):
            acc = jnp.zeros((CONV_ROWS, LANES), F32) + cb_ref[:, cols]
            for k in range(CONV_K):
                phase = (k + first) % sub
                rows = slice(r0 + k + first - phase, r0 + k + first - phase + CONV_ROWS)
                tap = buf[rows, cols] if phase == 0 else shift_scr[phase - 1, rows, cols]
                acc = acc + tap * cw_ref[k:k + 1, cols]
            conv_scr[r0:r0 + CONV_ROWS, cols] = acc
    y = _layernorm(conv_scr[...], cg_ref[...], cbeta_ref[...])
    y = (y * _sigmoid(y)).astype(BF16)
    o_ref[...] = _out_proj_ln(x_ref[...], c_ref[...], y, w_ref, g_ref, b_ref)


def _odd_out(c, d, x, conv_w, conv_b, conv_g, conv_beta, w_out, ln_g, ln_b):
    T = x.shape[0]
    tok = lambda n: pl.BlockSpec((TM, n), lambda i: (i, 0))
    prev, nxt = _halo_specs(T, TM, CONV_WIDTH)
    consts = (conv_w, conv_b, conv_g, conv_beta, w_out, ln_g, ln_b)
    return pl.pallas_call(
        _odd_out_kernel,
        grid=(T // TM,),
        in_specs=[tok(MLA_WIDTH), prev, tok(CONV_WIDTH), nxt, tok(D_MODEL)]
                 + [_const_spec(a.shape) for a in consts],
        out_specs=tok(D_MODEL),
        out_shape=jax.ShapeDtypeStruct((T, D_MODEL), F32),
        scratch_shapes=[pltpu.VMEM((TM + 2 * HALO, CONV_WIDTH), F32), pltpu.VMEM((TM, CONV_WIDTH), F32),
                        pltpu.VMEM((7, TM + HALO + 8, CONV_WIDTH), F32)],
        compiler_params=_params(),
        name="odd_out",
    )(c, d, d, d, x, *consts)


def _conv_ffn_kernel(xp_ref, x_ref, xn_ref, wu_ref, cw_ref, cb_ref, wd_ref, g_ref, b_ref, o_ref,
                     lhs, h_a, h_b, acc):
    _fill_halo_buffer(lhs, xp_ref, x_ref, xn_ref, BF16)

    n_blocks = TM_FFN // FF_ROWS
    margin = 8
    total = TM_FFN + 2 * HALO
    up_rows = [0] + [HALO + b * FF_ROWS for b in range(1, n_blocks)] + [total]

    def up(j, blk, h_scr):
        rows = slice(up_rows[blk], up_rows[blk + 1])
        h_scr[rows, :] = _dot(lhs[rows, :], wu_ref[j])

    def down(j, blk, h_scr, first=False):
        cw = cw_ref[j]
        lo = HALO + blk * FF_ROWS - margin
        n = FF_ROWS + 2 * margin
        h = h_scr[lo:lo + n, :]
        conv = (pltpu.roll(h, 1, 0) * cw[0:1, :] + h * cw[1:2, :] + pltpu.roll(h, n - 1, 0) * cw[2:3, :]
                + cb_ref[j])[margin:margin + FF_ROWS, :]
        gate = conv[:, :FF_CHUNK]
        act = (gate * _sigmoid(gate) * conv[:, FF_CHUNK:]).astype(BF16)
        part = _dot(act, wd_ref[j])
        rows = slice(blk * FF_ROWS, (blk + 1) * FF_ROWS)
        if first:
            acc[rows, :] = part
        else:
            acc[rows, :] += part

    def step(j, cur, nxt, first=False):
        for blk in range(n_blocks):
            if nxt is not None:
                up(j + 1, blk, nxt)
            down(j, blk, cur, first)

    for blk in range(n_blocks):
        up(0, blk, h_a)
    step(0, h_a, h_b, first=True)

    def pair(i, carry):
        j = 2 * i + 1
        step(j, h_b, h_a)
        step(j + 1, h_a, h_b)
        return carry

    assert N_FF_CHUNKS % 2 == 1 and N_FF_CHUNKS >= 3
    lax.fori_loop(0, (N_FF_CHUNKS - 3) // 2, pair, 0)
    step(N_FF_CHUNKS - 2, h_b, h_a)
    step(N_FF_CHUNKS - 1, h_a, None)
    o_ref[...] = _layernorm(ALPHA * x_ref[...] + acc[...], g_ref[...], b_ref[...])


def _conv_ffn(x, w_up, conv_w, conv_b, w_down, ln_g, ln_b):
    T = x.shape[0]
    tok = pl.BlockSpec((TM_FFN, D_MODEL), lambda i: (i, 0))
    prev, nxt = _halo_specs(T, TM_FFN, D_MODEL)
    consts = (w_up, conv_w, conv_b, w_down, ln_g, ln_b)
    h_buf = pltpu.VMEM((TM_FFN + 2 * HALO, 2 * FF_CHUNK), F32)
    return pl.pallas_call(
        _conv_ffn_kernel,
        grid=(T // TM_FFN,),
        in_specs=[prev, tok, nxt] + [_const_spec(a.shape) for a in consts],
        out_specs=tok,
        out_shape=jax.ShapeDtypeStruct((T, D_MODEL), F32),
        scratch_shapes=[pltpu.VMEM((TM_FFN + 2 * HALO, D_MODEL), BF16), h_buf, h_buf,
                        pltpu.VMEM((TM_FFN, D_MODEL), F32)],
        compiler_params=_params(),
        name="conv_ffn",
    )(x, x, x, *consts)


def _row(v):
    return v.reshape(1, -1).astype(F32)


def _nbr_bias(rpb):
    c = np.arange(GRID_W)
    c0 = np.clip(c - NA_WIN_C // 2, 0, GRID_W - NA_WIN_C)
    col_ok = (c[None, :] >= c0[:, None]) & (c[None, :] < c0[:, None] + NA_WIN_C)
    dc = np.clip(c[None, :] - c[:, None], -(NA_WIN_C - 1), NA_WIN_C - 1) + NA_WIN_C - 1
    row_tab = jnp.where(col_ok[None, None], rpb[:, :, dc] * LOG2E, NEG)
    row_tab = jnp.concatenate([row_tab, jnp.full_like(row_tab[:, :1], NEG)], axis=1)
    masked = 2 * NA_WIN_R - 1
    sel = np.full((3, NA_QROWS, NA_KROWS), masked)
    for cls, m in enumerate((0, 1, NA_BLOCKS - 1)):
        key_row0 = min(max(NA_QROWS * m - NA_WIN_R // 2, 0), ROWS - NA_KROWS)
        for i in range(NA_QROWS):
            qr = NA_QROWS * m + i
            r0 = min(max(qr - NA_WIN_R // 2, 0), ROWS - NA_WIN_R)
            for j in range(NA_KROWS):
                kr = key_row0 + j
                if r0 <= kr < r0 + NA_WIN_R:
                    sel[cls, i, j] = kr - qr + NA_WIN_R - 1
    b = row_tab[:, sel]
    b = b.transpose(1, 0, 2, 4, 3, 5)
    return b.reshape(3, NA_HEADS, NA_QBLK, NA_KBLK).astype(F32)


def _rope_tables():
    pos = jnp.arange(SEQ, dtype=F32)
    inv = ROPE_THETA ** (-jnp.arange(0, MLA_ROPE, 2, dtype=F32) / MLA_ROPE)
    ang = pos[:, None] * inv[None, :]
    cos, sin = jnp.cos(ang), jnp.sin(ang)
    csq = jnp.concatenate([jnp.tile(cos, (1, 4)), jnp.tile(sin, (1, 4))], axis=1)
    csk = jnp.concatenate([jnp.tile(cos, (1, 2)), jnp.tile(sin, (1, 2)),
                           jnp.zeros((SEQ, LANES - 2 * MLA_ROPE), F32)], axis=1)
    return csq, csk


def _rot_cols(w):
    half = w.shape[-1] // 2
    return jnp.concatenate([-w[..., half:], w[..., :half]], axis=-1)


def _odd_weights(w_in, w_uq, w_ukv):
    o1 = MLA_Q_LORA
    o2 = o1 + MLA_KV_LORA
    o3 = o2 + MLA_ROPE
    k_r = w_in[:, o2:o3]
    k_blk = jnp.concatenate([k_r, _rot_cols(k_r), jnp.zeros((D_MODEL, LANES - 2 * MLA_ROPE), F32)], axis=1)
    w_in2 = jnp.concatenate([w_in[:, :o2], k_blk, w_in[:, o3:]], axis=1)

    scale = (MLA_NOPE + MLA_ROPE) ** -0.5 * LOG2E
    wq =(w_uq * scale).reshape(MLA_Q_LORA, MLA_HEADS, MLA_NOPE + MLA_ROPE)
    nope, rope = wq[:, :, :MLA_NOPE], wq[:, :, MLA_NOPE:]
    rot = _rot_cols(rope)
    blocks = []
    for p in range(MLA_HEADS // 2):
        h0, h1 = 2 * p, 2 * p + 1
        blocks += [nope[:, h0], nope[:, h1], rope[:, h0], rope[:, h1], rot[:, h0], rot[:, h1]]
    wq2 = jnp.concatenate(blocks, axis=1)

    wkv = w_ukv.reshape(MLA_KV_LORA, MLA_HEADS, MLA_NOPE + MLA_V)
    wkv2 = jnp.concatenate([wkv[:, :, :MLA_NOPE].reshape(MLA_KV_LORA, -1),
                            wkv[:, :, MLA_NOPE:].reshape(MLA_KV_LORA, -1)], axis=1)
    return w_in2.astype(BF16), wq2.astype(BF16), wkv2.astype(BF16)


def _ffn_weights(w_up, conv_w, conv_b, w_down):
    def chunked(a):
        g = a[..., :D_FF].reshape(a.shape[:-1] + (N_FF_CHUNKS, FF_CHUNK))
        v = a[..., D_FF:].reshape(a.shape[:-1] + (N_FF_CHUNKS, FF_CHUNK))
        return jnp.moveaxis(jnp.concatenate([g, v], axis=-1), -2, 0)
    return (chunked(w_up).astype(BF16), chunked(conv_w).astype(F32), chunked(conv_b[None, :]).astype(F32),
            w_down.reshape(N_FF_CHUNKS, FF_CHUNK, D_MODEL).astype(BF16))


def _trunk(xs, w_in_even, rpb, sgu_ln_g, sgu_ln_b, sgu_w, sgu_b, w_out_even,
           w_in_odd, q_norm_g, w_uq, kv_norm_g, w_ukv, conv_w, conv_b, conv_ln_g, conv_ln_b,
           w_out_odd, ffn_w_up, ffn_conv_w, ffn_conv_b, ffn_w_down, ln1_g, ln1_b, ln2_g, ln2_b):
    shapes = [x.shape for x in xs]
    xs = [x.reshape(-1, D_MODEL) for x in xs]
    per_seq = lambda a: a.reshape(-1, SEQ, a.shape[-1])
    csq, csk = _rope_tables()
    for l in range(DEPTH):
        i = l // 2
        ln1 = (_row(ln1_g[l]), _row(ln1_b[l]))
        if l % 2 == 0:
            w_in = jnp.concatenate([w_in_even[i][:, :NA_WIDTH] * (NA_HEAD_DIM ** -0.5 * LOG2E),
                                    w_in_even[i][:, NA_WIDTH:]], axis=1).astype(BF16)
            sgu_ln = (_row(sgu_ln_g[i]), _row(sgu_ln_b[i]))
            bias = _nbr_bias(rpb[i])
            sgu_bias = jnp.broadcast_to(sgu_b[i][:, :, None], (SGU_GROUPS, SGU_CHUNK, LANES)).astype(F32)
            sw, w_out = sgu_w[i].astype(BF16), w_out_even[i].astype(BF16)

            def mixer(x):
                qkv, u, v = _even_in(x, w_in, *sgu_ln)
                a = _nbr_attn(per_seq(qkv), bias).reshape(-1, NA_WIDTH)
                return _even_out(a, u, v, x, sw, sgu_bias, w_out, *ln1)
        else:
            w_in, wq, wkv = _odd_weights(w_in_odd[i], w_uq[i], w_ukv[i])
            norms = (_row(q_norm_g[i]), _row(kv_norm_g[i]))
            cw = jnp.concatenate([conv_w[i], jnp.zeros((1, CONV_WIDTH), F32)], axis=0)
            conv = (cw, _row(conv_b[i]), _row(conv_ln_g[i]), _row(conv_ln_b[i]))
            w_out = w_out_odd[i].astype(BF16)

            def mixer(x):
                q, kn, kr, v, d = _odd_in(x, w_in, norms[0], wq, norms[1], wkv, csq, csk)
                c = _latent_attn(per_seq(q), per_seq(kn), per_seq(kr), per_seq(v)).reshape(-1, MLA_WIDTH)
                return _odd_out(c, d, x, *conv, w_out, *ln1)
        ffn = _ffn_weights(ffn_w_up[l], ffn_conv_w[l], ffn_conv_b[l], ffn_w_down[l])
        xs = [_conv_ffn(mixer(x), *ffn, _row(ln2_g[l]), _row(ln2_b[l])) for x in xs]
    return tuple(x.reshape(shape) for x, shape in zip(xs, shapes))


def kernel(x_prompt, x_sample, w_in_even, rpb, sgu_ln_g, sgu_ln_b, sgu_w, sgu_b, w_out_even, w_in_odd, q_norm_g, w_uq, kv_norm_g, w_ukv, conv_w, conv_b, conv_ln_g, conv_ln_b, w_out_odd, ffn_w_up, ffn_conv_w, ffn_conv_b, ffn_w_down, ln1_g, ln1_b, ln2_g, ln2_b):
    return _trunk([x_prompt, x_sample], w_in_even, rpb, sgu_ln_g, sgu_ln_b, sgu_w, sgu_b, w_out_even,
                  w_in_odd, q_norm_g, w_uq, kv_norm_g, w_ukv, conv_w, conv_b, conv_ln_g, conv_ln_b,
                  w_out_odd, ffn_w_up, ffn_conv_w, ffn_conv_b, ffn_w_down, ln1_g, ln1_b, ln2_g, ln2_b)
```

```python
import functools

import numpy as np
import jax
import jax.numpy as jnp
from jax import lax
from jax.experimental import pallas as pl
from jax.experimental.pallas import tpu as pltpu

F32 = jnp.float32
BF16 = jnp.bfloat16

D_MODEL = 1024
SEQ = 2048
DEPTH = 4
GRID_W = 64
ROWS = SEQ // GRID_W
NA_HEADS = 8
NA_HEAD_DIM = 64
NA_WIDTH = 512
NA_WIN_R = 8
NA_WIN_C = 16
SGU_WIDTH = 512
SGU_GROUPS = 4
SGU_CHUNK = 128
MLA_HEADS = 8
MLA_Q_LORA = 256
MLA_KV_LORA = 128
MLA_NOPE = 64
MLA_ROPE = 32
MLA_V = 64
MLA_WIDTH = 512
ROPE_THETA = 10000.0
CONV_WIDTH = 512
CONV_K = 31
D_FF = 2816
LN_EPS = 1e-5
RMS_EPS = 1e-6
ALPHA = (2 * DEPTH) ** 0.25

VMEM_LIMIT_BYTES = 56 * 1024 * 1024
LANES = 128
HALO = 16
TM = 512
TILES_PER_SEQ = SEQ // TM
TM_FFN = 1024
CONV_ROWS = 128
FF_ROWS = 256
FF_CHUNK = 256
N_FF_CHUNKS = D_FF // FF_CHUNK
NA_QROWS = 4
NA_KROWS = 12
NA_QBLK = NA_QROWS * GRID_W
NA_KBLK = NA_KROWS * GRID_W
NA_BLOCKS = ROWS // NA_QROWS
MLA_QBLK = 256
MLA_KTILE = 512
NA_KTILE = 256
LOG2E = 1.4426950408889634
NEG = -1e30


def _params(n_axes=1):
    return pltpu.CompilerParams(dimension_semantics=("arbitrary",) * n_axes,
                                vmem_limit_bytes=VMEM_LIMIT_BYTES)


def _const_spec(shape):
    nd = len(shape)
    return pl.BlockSpec(shape, lambda *_: (0,) * nd, pipeline_mode=pl.Buffered(1))


def _layernorm(z, g, b):
    mu = jnp.mean(z, axis=-1, keepdims=True)
    zc = z - mu
    var = jnp.mean(zc * zc, axis=-1, keepdims=True)
    return zc * lax.rsqrt(var + LN_EPS) * g + b


def _rmsnorm(z, g):
    return z * lax.rsqrt(jnp.mean(z * z, axis=-1, keepdims=True) + RMS_EPS) * g


def _sigmoid(z):
    return 1.0 / (1.0 + jnp.exp(-z))


def _dot(a, b):
    return jnp.dot(a, b, preferred_element_type=F32)


def _dot_nt(a, b):
    return lax.dot_general(a, b, (((1,), (1,)), ((), ())), preferred_element_type=F32)


def _even_in_kernel(x_ref, w_ref, g_ref, b_ref, qkv_ref, u_ref, v_ref):
    xb = x_ref[...].astype(BF16)
    qkv_ref[...] = _dot(xb, w_ref[:, :3 * NA_WIDTH]).astype(BF16)
    ug = _dot(xb, w_ref[:, 3 * NA_WIDTH:])
    u_ref[...] = jax.nn.gelu(ug[:, :SGU_WIDTH])
    v = _layernorm(jax.nn.gelu(ug[:, SGU_WIDTH:]), g_ref[...], b_ref[...])
    v_ref[...] = v.astype(BF16)


def _even_in(x, w, ln_g, ln_b):
    T = x.shape[0]
    tok = lambda n: pl.BlockSpec((TM, n), lambda i: (i, 0))
    return pl.pallas_call(
        _even_in_kernel,
        grid=(T // TM,),
        in_specs=[tok(D_MODEL), _const_spec(w.shape), _const_spec(ln_g.shape), _const_spec(ln_b.shape)],
        out_specs=[tok(3 * NA_WIDTH), tok(SGU_WIDTH), tok(SGU_WIDTH)],
        out_shape=[jax.ShapeDtypeStruct((T, 3 * NA_WIDTH), BF16),
                   jax.ShapeDtypeStruct((T, SGU_WIDTH), F32),
                   jax.ShapeDtypeStruct((T, SGU_WIDTH), BF16)],
        compiler_params=_params(),
        name="even_in",
    )(x, w, ln_g, ln_b)


def _rows(i, size, n=None):
    n = size if n is None else n
    start = i * size
    if not isinstance(i, int):
        start = pl.multiple_of(start, size)
    return pl.ds(start, n)


def _clip(v, lo, hi):
    return min(max(v, lo), hi) if isinstance(v, int) else jnp.clip(v, lo, hi)


def _pair_attention(n_blocks, n_tiles, qb, kt, q_heads, k_tile, vt_tile, bias_tile, write, s_bufs, m_bufs):
    half = LANES // 2

    def run(prod, cons):
        if prod is not None:
            ip, s_p, m_p = prod
            qh = q_heads(ip)
            mx = [jnp.full((8, qb), NEG, F32) for _ in range(2)]
        if cons is not None:
            ic, s_c, m_c = cons
            m = [m_c[h, 0:1, :] for h in range(2)]
            acc = [jnp.zeros((LANES, qb), F32) for _ in range(2)]
        for t in range(n_tiles):
            keys = slice(t * kt, (t + 1) * kt)
            if prod is not None:
                k = k_tile(ip, t)
                for h in range(2):
                    s = _dot_nt(k, qh[h])
                    b = bias_tile(ip, t, h)
                    if b is not None:
                        s = s + b
                    s_p[h, keys, :] = s
                    mx[h] = jnp.maximum(mx[h], jnp.max(s.reshape(kt // 8, 8, qb), axis=0))
            if cons is not None:
                for h in range(2):
                    p = jnp.exp2(s_c[h, keys, :] - m[h]).astype(BF16)
                    acc[h] = acc[h] + _dot(vt_tile(ic, t, h), p)
        if prod is not None:
            for h in range(2):
                m_p[h] = jnp.broadcast_to(jnp.max(mx[h], axis=0, keepdims=True), (8, qb))
        if cons is not None:
            out_t = jnp.concatenate([acc[0][:half] * (1.0 / acc[0][half:]),
                                     acc[1][half:] * (1.0 / acc[1][:half])], axis=0)
            write(ic, out_t.T)

    assert n_blocks % 2 == 0 and n_blocks >= 2
    buf_a = (s_bufs[0], m_bufs[0])
    buf_b = (s_bufs[1], m_bufs[1])
    run((0,) + buf_a, None)

    def pair(u, carry):
        i = 2 * u
        run((i + 1,) + buf_b, (i,) + buf_a)
        run((i + 2,) + buf_a, (i + 1,) + buf_b)
        return carry

    lax.fori_loop(0, n_blocks // 2 - 1, pair, 0)
    run((n_blocks - 1,) + buf_b, (n_blocks - 2,) + buf_a)
    run(None, (n_blocks - 1,) + buf_b)


def _head_rows_or_ones(vt, h):
    row = lax.broadcasted_iota(jnp.int32, (LANES, 1), 0)
    mine = (row < LANES // 2) if h == 0 else (row >= LANES // 2)
    return jnp.where(mine, vt, 1.0).astype(BF16)


def _nbr_attn_kernel(q_ref, k_ref, v_ref, bias_ref, o_ref, s_a, s_b, m_a, m_b, vt_scr):
    lane = lax.broadcasted_iota(jnp.int32, (1, LANES), 1)
    head_lanes = (lane < NA_HEAD_DIM, lane >= NA_HEAD_DIM)
    rows_per_chunk = LANES // GRID_W
    for c in range(SEQ // LANES):
        vt = v_ref[0, c * LANES:(c + 1) * LANES, :].astype(F32).T
        for h in range(2):
            vt_scr[h, c] = _head_rows_or_ones(vt, h)

    def key_row0(m):
        return _clip(NA_QROWS * m - NA_WIN_R // 2, 0, ROWS - NA_KROWS)

    def q_heads(m):
        q = q_ref[0, _rows(m, NA_QBLK), :]
        return [jnp.where(head_lanes[h], q, jnp.zeros_like(q)) for h in range(2)]

    def k_tile(m, t):
        return k_ref[0, _rows(key_row0(m) + t * (NA_KTILE // GRID_W), GRID_W, NA_KTILE), :]

    def vt_tile(m, t, h):
        c0 = key_row0(m) // rows_per_chunk + t * (NA_KTILE // LANES)
        return jnp.concatenate([vt_scr[h, c0 + c] for c in range(NA_KTILE // LANES)], axis=1)

    def bias_tile(m, t, h):
        if isinstance(m, int):
            cls = 0 if m == 0 else (2 if m == NA_BLOCKS - 1 else 1)
        else:
            cls = jnp.where(m == 0, 0, jnp.where(m == NA_BLOCKS - 1, 2, 1))
        return bias_ref[cls, h, t * NA_KTILE:(t + 1) * NA_KTILE, :]

    def write(m, out):
        o_ref[0, _rows(m, NA_QBLK), :] = out.astype(BF16)

    _pair_attention(NA_BLOCKS, NA_KBLK // NA_KTILE, NA_QBLK, NA_KTILE, q_heads, k_tile, vt_tile, bias_tile,
                    write, (s_a, s_b), (m_a, m_b))


def _nbr_attn(qkv, bias):
    B = qkv.shape[0]
    pairs = NA_HEADS // 2
    col = lambda off: pl.BlockSpec((1, SEQ, LANES), lambda p, b: (b, 0, off + p))
    return pl.pallas_call(
        _nbr_attn_kernel,
        grid=(pairs, B),
        in_specs=[col(0), col(pairs), col(2 * pairs),
                  pl.BlockSpec((3, 2, NA_KBLK, NA_QBLK), lambda p, b: (0, p, 0, 0))],
        out_specs=col(0),
        out_shape=jax.ShapeDtypeStruct((B, SEQ, NA_WIDTH), BF16),
        scratch_shapes=[pltpu.VMEM((2, NA_KBLK, NA_QBLK), F32)] * 2 + [pltpu.VMEM((2, 8, NA_QBLK), F32)] * 2
                       + [pltpu.VMEM((2, SEQ // LANES, LANES, LANES), BF16)],
        compiler_params=_params(2),
        name="nbr_attn",
    )(qkv, qkv, qkv, bias)


def _out_proj_ln(x, mixer_lo, mixer_hi, w_ref, g_ref, b_ref):
    half = w_ref.shape[0] // 2
    m = _dot(mixer_lo, w_ref[:half, :]) + _dot(mixer_hi, w_ref[half:, :])
    return _layernorm(ALPHA * x + m, g_ref[...], b_ref[...])


def _even_out_kernel(a_ref, u_ref, v_ref, x_ref, sw_ref, sb_ref, w_ref, g_ref, b_ref, o_ref, gate_scr):
    for c in range(TM // SGU_CHUNK):
        rows = slice(c * SGU_CHUNK, (c + 1) * SGU_CHUNK)
        for g in range(SGU_GROUPS):
            cols = slice(g * LANES, (g + 1) * LANES)
            mixed = _dot(sw_ref[g], v_ref[rows, cols]) + sb_ref[g]
            gate_scr[rows, cols] = (u_ref[rows, cols] * mixed).astype(BF16)
    o_ref[...] = _out_proj_ln(x_ref[...], a_ref[...], gate_scr[...], w_ref, g_ref, b_ref)


def _even_out(a, u, v, x, sgu_w, sgu_bias, w_out, ln_g, ln_b):
    T = x.shape[0]
    tok = lambda n: pl.BlockSpec((TM, n), lambda i: (i, 0))
    consts = (sgu_w, sgu_bias, w_out, ln_g, ln_b)
    return pl.pallas_call(
        _even_out_kernel,
        grid=(T // TM,),
        in_specs=[tok(NA_WIDTH), tok(SGU_WIDTH), tok(SGU_WIDTH), tok(D_MODEL)]
                 + [_const_spec(c.shape) for c in consts],
        out_specs=tok(D_MODEL),
        out_shape=jax.ShapeDtypeStruct((T, D_MODEL), F32),
        scratch_shapes=[pltpu.VMEM((TM, SGU_WIDTH), BF16)],
        compiler_params=_params(),
        name="even_out",
    )(a, u, v, x, *consts)


def _odd_in_kernel(x_ref, w_ref, qg_ref, wq_ref, kg_ref, wkv_ref, csq_ref, csk_ref,
                   q_ref, kn_ref, kr_ref, v_ref, d_ref):
    xb = x_ref[...].astype(BF16)
    h = _dot(xb, w_ref[...])
    o1 = MLA_Q_LORA
    o2 = o1 + MLA_KV_LORA
    o3 = o2 + LANES
    o4 = o3 + CONV_WIDTH

    cq = _rmsnorm(h[:, :o1], qg_ref[...]).astype(BF16)
    qf = _dot(cq, wq_ref[...])
    csq = csq_ref[...]
    for p in range(MLA_HEADS // 2):
        base = 2 * LANES * p
        q_ref[:, base:base + LANES] = qf[:, base:base + LANES].astype(BF16)
        prod = qf[:, base + LANES:base + 2 * LANES] * csq
        q_ref[:, base + LANES:base + 2 * LANES] = (prod + pltpu.roll(prod, 64, 1)).astype(BF16)

    ckv = _rmsnorm(h[:, o1:o2], kg_ref[...]).astype(BF16)
    kv = _dot(ckv, wkv_ref[...])
    kn_ref[...] = kv[:, :MLA_WIDTH].astype(BF16)
    v_ref[...] = kv[:, MLA_WIDTH:].astype(BF16)

    prod = h[:, o2:o3] * csk_ref[...]
    roped = prod + pltpu.roll(prod, LANES - MLA_ROPE, 1)
    lane = lax.broadcasted_iota(jnp.int32, (1, LANES), 1)
    kr_ref[...] = jnp.where(lane < MLA_ROPE, roped, pltpu.roll(roped, MLA_ROPE, 1)).astype(BF16)

    d_ref[...] = h[:, o3:o4] * _sigmoid(h[:, o4:])


def _odd_in(x, w, q_g, wq, kv_g, wkv, csq, csk):
    T = x.shape[0]
    tok = lambda n: pl.BlockSpec((TM, n), lambda i: (i, 0))
    pos = pl.BlockSpec((TM, LANES), lambda i: (i % TILES_PER_SEQ, 0))
    return pl.pallas_call(
        _odd_in_kernel,
        grid=(T // TM,),
        in_specs=[tok(D_MODEL), _const_spec(w.shape), _const_spec(q_g.shape), _const_spec(wq.shape),
                  _const_spec(kv_g.shape), _const_spec(wkv.shape), pos, pos],
        out_specs=[tok(2 * MLA_WIDTH), tok(MLA_WIDTH), tok(LANES), tok(MLA_WIDTH), tok(CONV_WIDTH)],
        out_shape=[jax.ShapeDtypeStruct((T, 2 * MLA_WIDTH), BF16),
                   jax.ShapeDtypeStruct((T, MLA_WIDTH), BF16),
                   jax.ShapeDtypeStruct((T, LANES), BF16),
                   jax.ShapeDtypeStruct((T, MLA_WIDTH), BF16),
                   jax.ShapeDtypeStruct((T, CONV_WIDTH), F32)],
        compiler_params=_params(),
        name="odd_in",
    )(x, w, q_g, wq, kv_g, wkv, csq, csk)


def _latent_attn_kernel(q_ref, kn_ref, kr_ref, v_ref, o_ref, k_scr, vt_scr, s_a, s_b, m_a, m_b):
    lane2 = lax.broadcasted_iota(jnp.int32, (1, 2 * LANES), 1)
    k_scr[:, :LANES] = kn_ref[0]
    k_scr[:, LANES:] = kr_ref[0]
    for c in range(SEQ // MLA_KTILE):
        keys = slice(c * MLA_KTILE, (c + 1) * MLA_KTILE)
        vt = v_ref[0, keys, :].astype(F32).T
        for h in range(2):
            vt_scr[h, :, keys] = _head_rows_or_ones(vt, h)
    q_lanes = []
    for h in range(2):
        nope = (lane2 >= h * MLA_NOPE) & (lane2 < (h + 1) * MLA_NOPE)
        rope = (lane2 >= LANES + h * MLA_ROPE) & (lane2 < LANES + (h + 1) * MLA_ROPE)
        q_lanes.append(nope | rope)

    def q_heads(i):
        q = q_ref[0, _rows(i, MLA_QBLK), :]
        return [jnp.where(q_lanes[h], q, jnp.zeros_like(q)) for h in range(2)]

    def k_tile(i, t):
        return k_scr[t * MLA_KTILE:(t + 1) * MLA_KTILE, :]

    def vt_tile(i, t, h):
        return vt_scr[h, :, t * MLA_KTILE:(t + 1) * MLA_KTILE]

    def write(i, out):
        o_ref[0, _rows(i, MLA_QBLK), :] = out.astype(BF16)

    _pair_attention(SEQ // MLA_QBLK, SEQ // MLA_KTILE, MLA_QBLK, MLA_KTILE, q_heads, k_tile, vt_tile,
                    lambda i, t, h: None, write, (s_a, s_b), (m_a, m_b))


def _latent_attn(q, kn, kr, v):
    B = q.shape[0]
    pairs = MLA_HEADS // 2
    col = lambda width: pl.BlockSpec((1, SEQ, width), lambda b, p: (b, 0, p))
    return pl.pallas_call(
        _latent_attn_kernel,
        grid=(B, pairs),
        in_specs=[col(2 * LANES), col(LANES), pl.BlockSpec((1, SEQ, LANES), lambda b, p: (b, 0, 0)), col(LANES)],
        out_specs=col(LANES),
        out_shape=jax.ShapeDtypeStruct((B, SEQ, MLA_WIDTH), BF16),
        scratch_shapes=[pltpu.VMEM((SEQ, 2 * LANES), BF16), pltpu.VMEM((2, LANES, SEQ), BF16)]
                       + [pltpu.VMEM((2, SEQ, MLA_QBLK), F32)] * 2 + [pltpu.VMEM((2, 8, MLA_QBLK), F32)] * 2,
        compiler_params=_params(2),
        name="latent_attn",
    )(q, kn, kr, v)


def _fill_halo_buffer(buf, prev_ref, cur_ref, next_ref, dtype):
    tm = cur_ref.shape[0]
    tiles_per_seq = SEQ // tm
    i = pl.program_id(0) % tiles_per_seq
    prev = prev_ref[...]
    nxt = next_ref[...]
    buf[:HALO, :] = jnp.where(i == 0, jnp.zeros_like(prev), prev).astype(dtype)
    buf[HALO:HALO + tm, :] = cur_ref[...].astype(dtype)
    buf[HALO + tm:, :] = jnp.where(i == tiles_per_seq - 1, jnp.zeros_like(nxt), nxt).astype(dtype)


def _halo_specs(T, tm, n):
    per = tm // HALO
    last = T // HALO - 1
    return (pl.BlockSpec((HALO, n), lambda i: (jnp.maximum(i * per - 1, 0), 0)),
            pl.BlockSpec((HALO, n), lambda i: (jnp.minimum((i + 1) * per, last), 0)))


def _odd_out_kernel(c_ref, dp_ref, d_ref, dn_ref, x_ref, cw_ref, cb_ref, cg_ref, cbeta_ref,
                    w_ref, g_ref, b_ref, o_ref, buf, conv_scr, shift_scr):
    _fill_halo_buffer(buf, dp_ref, d_ref, dn_ref, F32)
    first = HALO - CONV_K // 2
    sub = 8
    reach = (first + CONV_K - 1) // sub * sub
    for phase in range(1, sub):
        shift_scr[phase - 1] = buf[phase:phase + TM + reach, :]
    for c in range(CONV_WIDTH // LANES):
        cols = slice(c * LANES, (c + 1) * LANES)
        for r0 in range(0, TM, CONV_ROWS):
            acc = jnp.zeros((CONV_ROWS, LANES), F32) + cb_ref[:, cols]
            for k in range(CONV_K):
                phase = (k + first) % sub
                rows = slice(r0 + k + first - phase, r0 + k + first - phase + CONV_ROWS)
                tap = buf[rows, cols] if phase == 0 else shift_scr[phase - 1, rows, cols]
                acc = acc + tap * cw_ref[k:k + 1, cols]
            conv_scr[r0:r0 + CONV_ROWS, cols] = acc
    y = _layernorm(conv_scr[...], cg_ref[...], cbeta_ref[...])
    y = (y * _sigmoid(y)).astype(BF16)
    o_ref[...] = _out_proj_ln(x_ref[...], c_ref[...], y, w_ref, g_ref, b_ref)


def _odd_out(c, d, x, conv_w, conv_b, conv_g, conv_beta, w_out, ln_g, ln_b):
    T = x.shape[0]
    tok = lambda n: pl.BlockSpec((TM, n), lambda i: (i, 0))
    prev, nxt = _halo_specs(T, TM, CONV_WIDTH)
    consts = (conv_w, conv_b, conv_g, conv_beta, w_out, ln_g, ln_b)
    return pl.pallas_call(
        _odd_out_kernel,
        grid=(T // TM,),
        in_specs=[tok(MLA_WIDTH), prev, tok(CONV_WIDTH), nxt, tok(D_MODEL)]
                 + [_const_spec(a.shape) for a in consts],
        out_specs=tok(D_MODEL),
        out_shape=jax.ShapeDtypeStruct((T, D_MODEL), F32),
        scratch_shapes=[pltpu.VMEM((TM + 2 * HALO, CONV_WIDTH), F32), pltpu.VMEM((TM, CONV_WIDTH), F32),
                        pltpu.VMEM((7, TM + HALO + 8, CONV_WIDTH), F32)],
        compiler_params=_params(),
        name="odd_out",
    )(c, d, d, d, x, *consts)


def _conv_ffn_kernel(xp_ref, x_ref, xn_ref, wu_ref, cw_ref, cb_ref, wd_ref, g_ref, b_ref, o_ref,
                     lhs, h_a, h_b, acc):
    _fill_halo_buffer(lhs, xp_ref, x_ref, xn_ref, BF16)

    n_blocks = TM_FFN // FF_ROWS
    margin = 8
    total = TM_FFN + 2 * HALO
    up_rows = [0] + [HALO + b * FF_ROWS for b in range(1, n_blocks)] + [total]

    def up(j, blk, h_scr):
        rows = slice(up_rows[blk], up_rows[blk + 1])
        h_scr[rows, :] = _dot(lhs[rows, :], wu_ref[j])

    def down(j, blk, h_scr, first=False):
        cw = cw_ref[j]
        lo = HALO + blk * FF_ROWS - margin
        n = FF_ROWS + 2 * margin
        h = h_scr[lo:lo + n, :]
        conv = (pltpu.roll(h, 1, 0) * cw[0:1, :] + h * cw[1:2, :] + pltpu.roll(h, n - 1, 0) * cw[2:3, :]
                + cb_ref[j])[margin:margin + FF_ROWS, :]
        gate = conv[:, :FF_CHUNK]
        act = (gate * (1.0 / (1.0 + jnp.exp2(gate))) * conv[:, FF_CHUNK:]).astype(BF16)
        part = _dot(act, wd_ref[j])
        rows = slice(blk * FF_ROWS, (blk + 1) * FF_ROWS)
        if first:
            acc[rows, :] = part
        else:
            acc[rows, :] += part

    def step(j, cur, nxt, first=False):
        for blk in range(n_blocks):
            if nxt is not None:
                up(j + 1, blk, nxt)
            down(j, blk, cur, first)

    for blk in range(n_blocks):
        up(0, blk, h_a)
    step(0, h_a, h_b, first=True)

    def pair(i, carry):
        j = 2 * i + 1
        step(j, h_b, h_a)
        step(j + 1, h_a, h_b)
        return carry

    assert N_FF_CHUNKS % 2 == 1 and N_FF_CHUNKS >= 3
    lax.fori_loop(0, (N_FF_CHUNKS - 3) // 2, pair, 0)
    step(N_FF_CHUNKS - 2, h_b, h_a)
    step(N_FF_CHUNKS - 1, h_a, None)
    o_ref[...] = _layernorm(ALPHA * x_ref[...] + acc[...], g_ref[...], b_ref[...])


def _conv_ffn(x, w_up, conv_w, conv_b, w_down, ln_g, ln_b):
    T = x.shape[0]
    tok = pl.BlockSpec((TM_FFN, D_MODEL), lambda i: (i, 0))
    prev, nxt = _halo_specs(T, TM_FFN, D_MODEL)
    consts = (w_up, conv_w, conv_b, w_down, ln_g, ln_b)
    h_buf = pltpu.VMEM((TM_FFN + 2 * HALO, 2 * FF_CHUNK), F32)
    return pl.pallas_call(
        _conv_ffn_kernel,
        grid=(T // TM_FFN,),
        in_specs=[prev, tok, nxt] + [_const_spec(a.shape) for a in consts],
        out_specs=tok,
        out_shape=jax.ShapeDtypeStruct((T, D_MODEL), F32),
        scratch_shapes=[pltpu.VMEM((TM_FFN + 2 * HALO, D_MODEL), BF16), h_buf, h_buf,
                        pltpu.VMEM((TM_FFN, D_MODEL), F32)],
        compiler_params=_params(),
        name="conv_ffn",
    )(x, x, x, *consts)


def _row(v):
    return v.reshape(1, -1).astype(F32)


def _nbr_bias(rpb):
    c = np.arange(GRID_W)
    c0 = np.clip(c - NA_WIN_C // 2, 0, GRID_W - NA_WIN_C)
    col_ok = (c[None, :] >= c0[:, None]) & (c[None, :] < c0[:, None] + NA_WIN_C)
    dc = np.clip(c[None, :] - c[:, None], -(NA_WIN_C - 1), NA_WIN_C - 1) + NA_WIN_C - 1
    row_tab = jnp.where(col_ok[None, None], rpb[:, :, dc] * LOG2E, NEG)
    row_tab = jnp.concatenate([row_tab, jnp.full_like(row_tab[:, :1], NEG)], axis=1)
    masked = 2 * NA_WIN_R - 1
    sel = np.full((3, NA_QROWS, NA_KROWS), masked)
    for cls, m in enumerate((0, 1, NA_BLOCKS - 1)):
        key_row0 = min(max(NA_QROWS * m - NA_WIN_R // 2, 0), ROWS - NA_KROWS)
        for i in range(NA_QROWS):
            qr = NA_QROWS * m + i
            r0 = min(max(qr - NA_WIN_R // 2, 0), ROWS - NA_WIN_R)
            for j in range(NA_KROWS):
                kr = key_row0 + j
                if r0 <= kr < r0 + NA_WIN_R:
                    sel[cls, i, j] = kr - qr + NA_WIN_R - 1
    b = row_tab[:, sel]
    b = b.transpose(1, 0, 3, 5, 2, 4)
    return b.reshape(3, NA_HEADS, NA_KBLK, NA_QBLK).astype(F32)


def _rope_tables():
    pos = jnp.arange(SEQ, dtype=F32)
    inv = ROPE_THETA ** (-jnp.arange(0, MLA_ROPE, 2, dtype=F32) / MLA_ROPE)
    ang = pos[:, None] * inv[None, :]
    cos, sin = jnp.cos(ang), jnp.sin(ang)
    csq = jnp.concatenate([jnp.tile(cos, (1, 4)), jnp.tile(sin, (1, 4))], axis=1)
    csk = jnp.concatenate([jnp.tile(cos, (1, 2)), jnp.tile(sin, (1, 2)),
                           jnp.zeros((SEQ, LANES - 2 * MLA_ROPE), F32)], axis=1)
    return csq, csk


def _rot_cols(w):
    half = w.shape[-1] // 2
    return jnp.concatenate([-w[..., half:], w[..., :half]], axis=-1)


def _odd_weights(w_in, w_uq, w_ukv):
    o1 = MLA_Q_LORA
    o2 = o1 + MLA_KV_LORA
    o3 = o2 + MLA_ROPE
    k_r = w_in[:, o2:o3]
    k_blk = jnp.concatenate([k_r, _rot_cols(k_r), jnp.zeros((D_MODEL, LANES - 2 * MLA_ROPE), F32)], axis=1)
    w_in2 = jnp.concatenate([w_in[:, :o2], k_blk, w_in[:, o3:]], axis=1)

    scale = (MLA_NOPE + MLA_ROPE) ** -0.5 * LOG2E
    wq =(w_uq * scale).reshape(MLA_Q_LORA, MLA_HEADS, MLA_NOPE + MLA_ROPE)
    nope, rope = wq[:, :, :MLA_NOPE], wq[:, :, MLA_NOPE:]
    rot = _rot_cols(rope)
    blocks = []
    for p in range(MLA_HEADS // 2):
        h0, h1 = 2 * p, 2 * p + 1
        blocks += [nope[:, h0], nope[:, h1], rope[:, h0], rope[:, h1], rot[:, h0], rot[:, h1]]
    wq2 = jnp.concatenate(blocks, axis=1)

    wkv = w_ukv.reshape(MLA_KV_LORA, MLA_HEADS, MLA_NOPE + MLA_V)
    wkv2 = jnp.concatenate([wkv[:, :, :MLA_NOPE].reshape(MLA_KV_LORA, -1),
                            wkv[:, :, MLA_NOPE:].reshape(MLA_KV_LORA, -1)], axis=1)
    return w_in2.astype(BF16), wq2.astype(BF16), wkv2.astype(BF16)


def _ffn_weights(w_up, conv_w, conv_b, w_down):
    def chunked(a):
        g = a[..., :D_FF].reshape(a.shape[:-1] + (N_FF_CHUNKS, FF_CHUNK))
        v = a[..., D_FF:].reshape(a.shape[:-1] + (N_FF_CHUNKS, FF_CHUNK))
        return jnp.moveaxis(jnp.concatenate([g, v], axis=-1), -2, 0)
    col_scale = jnp.concatenate([jnp.full((D_FF,), -LOG2E, F32), jnp.full((D_FF,), -1.0 / LOG2E, F32)])
    return (chunked(w_up).astype(BF16), chunked(conv_w * col_scale).astype(F32),
            chunked((conv_b * col_scale)[None, :]).astype(F32),
            w_down.reshape(N_FF_CHUNKS, FF_CHUNK, D_MODEL).astype(BF16))


def _trunk(xs, w_in_even, rpb, sgu_ln_g, sgu_ln_b, sgu_w, sgu_b, w_out_even,
           w_in_odd, q_norm_g, w_uq, kv_norm_g, w_ukv, conv_w, conv_b, conv_ln_g, conv_ln_b,
           w_out_odd, ffn_w_up, ffn_conv_w, ffn_conv_b, ffn_w_down, ln1_g, ln1_b, ln2_g, ln2_b):
    shapes = [x.shape for x in xs]
    xs = [x.reshape(-1, D_MODEL) for x in xs]
    per_seq = lambda a: a.reshape(-1, SEQ, a.shape[-1])
    csq, csk = _rope_tables()
    for l in range(DEPTH):
        i = l // 2
        ln1 = (_row(ln1_g[l]), _row(ln1_b[l]))
        if l % 2 == 0:
            w_in = jnp.concatenate([w_in_even[i][:, :NA_WIDTH] * (NA_HEAD_DIM ** -0.5 * LOG2E),
                                    w_in_even[i][:, NA_WIDTH:]], axis=1).astype(BF16)
            sgu_ln = (_row(sgu_ln_g[i]), _row(sgu_ln_b[i]))
            bias = _nbr_bias(rpb[i])
            sgu_bias = jnp.broadcast_to(sgu_b[i][:, :, None], (SGU_GROUPS, SGU_CHUNK, LANES)).astype(F32)
            sw, w_out = sgu_w[i].astype(BF16), w_out_even[i].astype(BF16)

            def mixer(x):
                qkv, u, v = _even_in(x, w_in, *sgu_ln)
                a = _nbr_attn(per_seq(qkv), bias).reshape(-1, NA_WIDTH)
                return _even_out(a, u, v, x, sw, sgu_bias, w_out, *ln1)
        else:
            w_in, wq, wkv = _odd_weights(w_in_odd[i], w_uq[i], w_ukv[i])
            norms = (_row(q_norm_g[i]), _row(kv_norm_g[i]))
            cw = jnp.concatenate([conv_w[i], jnp.zeros((1, CONV_WIDTH), F32)], axis=0)
            conv = (cw, _row(conv_b[i]), _row(conv_ln_g[i]), _row(conv_ln_b[i]))
            w_out = w_out_odd[i].astype(BF16)

            def mixer(x):
                q, kn, kr, v, d = _odd_in(x, w_in, norms[0], wq, norms[1], wkv, csq, csk)
                c = _latent_attn(per_seq(q), per_seq(kn), per_seq(kr), per_seq(v)).reshape(-1, MLA_WIDTH)
                return _odd_out(c, d, x, *conv, w_out, *ln1)
        ffn = _ffn_weights(ffn_w_up[l], ffn_conv_w[l], ffn_conv_b[l], ffn_w_down[l])
        xs = [_conv_ffn(mixer(x), *ffn, _row(ln2_g[l]), _row(ln2_b[l])) for x in xs]
    return tuple(x.reshape(shape) for x, shape in zip(xs, shapes))


def kernel(x_prompt, x_sample, w_in_even, rpb, sgu_ln_g, sgu_ln_b, sgu_w, sgu_b, w_out_even, w_in_odd, q_norm_g, w_uq, kv_norm_g, w_ukv, conv_w, conv_b, conv_ln_g, conv_ln_b, w_out_odd, ffn_w_up, ffn_conv_w, ffn_conv_b, ffn_w_down, ln1_g, ln1_b, ln2_g, ln2_b):
    return _trunk([x_prompt, x_sample], w_in_even, rpb, sgu_ln_g, sgu_ln_b, sgu_w, sgu_b, w_out_even,
                  w_in_odd, q_norm_g, w_uq, kv_norm_g, w_ukv, conv_w, conv_b, conv_ln_g, conv_ln_b,
                  w_out_odd, ffn_w_up, ffn_conv_w, ffn_conv_b, ffn_w_down, ln1_g, ln1_b, ln2_g, ln2_b)
```

```python
import functools

import numpy as np
import jax
import jax.numpy as jnp
from jax import lax
from jax.experimental import pallas as pl
from jax.experimental.pallas import tpu as pltpu

F32 = jnp.float32
BF16 = jnp.bfloat16

D_MODEL = 1024
SEQ = 2048
DEPTH = 4
GRID_W = 64
ROWS = SEQ // GRID_W
NA_HEADS = 8
NA_HEAD_DIM = 64
NA_WIDTH = 512
NA_WIN_R = 8
NA_WIN_C = 16
SGU_WIDTH = 512
SGU_GROUPS = 4
SGU_CHUNK = 128
MLA_HEADS = 8
MLA_Q_LORA = 256
MLA_KV_LORA = 128
MLA_NOPE = 64
MLA_ROPE = 32
MLA_V = 64
MLA_WIDTH = 512
ROPE_THETA = 10000.0
CONV_WIDTH = 512
CONV_K = 31
D_FF = 2816
LN_EPS = 1e-5
RMS_EPS = 1e-6
ALPHA = (2 * DEPTH) ** 0.25

VMEM_LIMIT_BYTES = 56 * 1024 * 1024
LANES = 128
HALO = 16
TM = 1024
TM_CONV = 512
TILES_PER_SEQ = SEQ // TM
TM_FFN = 1024
CONV_ROWS = 128
FF_ROWS = 512
FF_CHUNK = 256
N_FF_CHUNKS = D_FF // FF_CHUNK
NA_QROWS = 4
NA_KROWS = 12
NA_QBLK = NA_QROWS * GRID_W
NA_KBLK = NA_KROWS * GRID_W
NA_BLOCKS = ROWS // NA_QROWS
MLA_QBLK = 512
MLA_KTILE = 256
NA_KTILE = 256
LOG2E = 1.4426950408889634
NEG = -1e30


def _params(n_axes=1):
    return pltpu.CompilerParams(dimension_semantics=("arbitrary",) * n_axes,
                                vmem_limit_bytes=VMEM_LIMIT_BYTES)


def _const_spec(shape):
    nd = len(shape)
    return pl.BlockSpec(shape, lambda *_: (0,) * nd, pipeline_mode=pl.Buffered(1))


def _layernorm(z, g, b):
    mu = jnp.mean(z, axis=-1, keepdims=True)
    zc = z - mu
    var = jnp.mean(zc * zc, axis=-1, keepdims=True)
    return zc * lax.rsqrt(var + LN_EPS) * g + b


def _rmsnorm(z, g):
    return z * lax.rsqrt(jnp.mean(z * z, axis=-1, keepdims=True) + RMS_EPS) * g


def _sigmoid(z):
    return 1.0 / (1.0 + jnp.exp(-z))


def _dot(a, b):
    return jnp.dot(a, b, preferred_element_type=F32)


def _dot_nt(a, b):
    return lax.dot_general(a, b, (((1,), (1,)), ((), ())), preferred_element_type=F32)


def _even_in_kernel(x_ref, w_ref, g_ref, b_ref, qkv_ref, u_ref, v_ref):
    xb = x_ref[...].astype(BF16)
    qkv_ref[...] = _dot(xb, w_ref[:, :3 * NA_WIDTH]).astype(BF16)
    ug = _dot(xb, w_ref[:, 3 * NA_WIDTH:])
    u_ref[...] = jax.nn.gelu(ug[:, :SGU_WIDTH])
    v = _layernorm(jax.nn.gelu(ug[:, SGU_WIDTH:]), g_ref[...], b_ref[...])
    v_ref[...] = v.astype(BF16)


def _even_in(x, w, ln_g, ln_b):
    T = x.shape[0]
    tok = lambda n: pl.BlockSpec((TM, n), lambda i: (i, 0))
    return pl.pallas_call(
        _even_in_kernel,
        grid=(T // TM,),
        in_specs=[tok(D_MODEL), _const_spec(w.shape), _const_spec(ln_g.shape), _const_spec(ln_b.shape)],
        out_specs=[tok(3 * NA_WIDTH), tok(SGU_WIDTH), tok(SGU_WIDTH)],
        out_shape=[jax.ShapeDtypeStruct((T, 3 * NA_WIDTH), BF16),
                   jax.ShapeDtypeStruct((T, SGU_WIDTH), F32),
                   jax.ShapeDtypeStruct((T, SGU_WIDTH), BF16)],
        compiler_params=_params(),
        name="even_in",
    )(x, w, ln_g, ln_b)


def _rows(i, size, n=None):
    n = size if n is None else n
    start = i * size
    if not isinstance(i, int):
        start = pl.multiple_of(start, size)
    return pl.ds(start, n)


def _clip(v, lo, hi):
    return min(max(v, lo), hi) if isinstance(v, int) else jnp.clip(v, lo, hi)


def _pair_attention(n_blocks, n_tiles, qb, kt, q_heads, k_tile, vt_tile, bias_tile, write, s_bufs, m_bufs):
    half = LANES // 2

    def run(prod, cons):
        if prod is not None:
            ip, s_p, m_p = prod
            qh = q_heads(ip)
            mx = [jnp.full((8, qb), NEG, F32) for _ in range(2)]
        if cons is not None:
            ic, s_c, m_c = cons
            m = [m_c[h, 0:1, :] for h in range(2)]
            acc = [jnp.zeros((LANES, qb), F32) for _ in range(2)]
        for t in range(n_tiles):
            keys = slice(t * kt, (t + 1) * kt)
            if prod is not None:
                k = k_tile(ip, t)
                for h in range(2):
                    s = _dot_nt(k, qh[h])
                    b = bias_tile(ip, t, h)
                    if b is not None:
                        s = s + b
                    s_p[h, keys, :] = s
                    mx[h] = jnp.maximum(mx[h], jnp.max(s.reshape(kt // 8, 8, qb), axis=0))
            if cons is not None:
                for h in range(2):
                    p = jnp.exp2(s_c[h, keys, :] - m[h]).astype(BF16)
                    acc[h] = acc[h] + _dot(vt_tile(ic, t, h), p)
        if prod is not None:
            for h in range(2):
                m_p[h] = jnp.broadcast_to(jnp.max(mx[h], axis=0, keepdims=True), (8, qb))
        if cons is not None:
            out_t = jnp.concatenate([acc[0][:half] * (1.0 / acc[0][half:]),
                                     acc[1][half:] * (1.0 / acc[1][:half])], axis=0)
            write(ic, out_t.T)

    _two_buffer_pipeline(n_blocks, run, s_bufs, m_bufs)


def _two_buffer_pipeline(n_blocks, run, s_bufs, m_bufs):
    assert n_blocks % 2 == 0 and n_blocks >= 2
    buf_a = (s_bufs[0], m_bufs[0])
    buf_b = (s_bufs[1], m_bufs[1])
    run((0,) + buf_a, None)

    def pair(u, carry):
        i = 2 * u
        run((i + 1,) + buf_b, (i,) + buf_a)
        run((i + 2,) + buf_a, (i + 1,) + buf_b)
        return carry

    lax.fori_loop(0, n_blocks // 2 - 1, pair, 0)
    run((n_blocks - 1,) + buf_b, (n_blocks - 2,) + buf_a)
    run(None, (n_blocks - 1,) + buf_b)


def _pair_attention_rows(n_blocks, n_tiles, qb, kt, q_heads, k_tile, v_tile, write, s_bufs, m_bufs):
    lane_chunks = kt // LANES

    def run(prod, cons):
        if prod is not None:
            ip, s_p, m_p = prod
            qh = q_heads(ip)
            mx = [jnp.full((qb, LANES), NEG, F32) for _ in range(2)]
        if cons is not None:
            ic, s_c, m_c = cons
            m = [m_c[h] for h in range(2)]
            acc = [jnp.zeros((qb, LANES), F32) for _ in range(2)]
        for t in range(n_tiles):
            if prod is not None:
                k = k_tile(ip, t)
                for h in range(2):
                    s = _dot_nt(qh[h], k)
                    s_p[h, :, t * kt:(t + 1) * kt] = s
                    for c in range(lane_chunks):
                        mx[h] = jnp.maximum(mx[h], s[:, c * LANES:(c + 1) * LANES])
            if cons is not None:
                for h in range(2):
                    p = [jnp.exp2(s_c[h, :, t * kt + c * LANES:t * kt + (c + 1) * LANES] - m[h])
                         for c in range(lane_chunks)]
                    p = jnp.concatenate(p, axis=1).astype(BF16)
                    acc[h] = acc[h] + _dot(p, v_tile(ic, t, h))
        if prod is not None:
            for h in range(2):
                m_p[h] = jnp.broadcast_to(jnp.max(mx[h], axis=-1, keepdims=True), (qb, LANES))
        if cons is not None:
            lane = lax.broadcasted_iota(jnp.int32, (1, LANES), 1)
            halves = [acc[h] * pltpu.roll(1.0 / acc[h], LANES // 2, 1) for h in range(2)]
            write(ic, jnp.where(lane < LANES // 2, halves[0], halves[1]))

    _two_buffer_pipeline(n_blocks, run, s_bufs, m_bufs)


def _head_rows_or_ones(vt, h):
    row = lax.broadcasted_iota(jnp.int32, (LANES, 1), 0)
    mine = (row < LANES // 2) if h == 0 else (row >= LANES // 2)
    return jnp.where(mine, vt, 1.0).astype(BF16)


def _nbr_attn_kernel(q_ref, k_ref, v_ref, bias_ref, o_ref, s_a, s_b, m_a, m_b, vt_scr):
    lane = lax.broadcasted_iota(jnp.int32, (1, LANES), 1)
    head_lanes = (lane < NA_HEAD_DIM, lane >= NA_HEAD_DIM)
    rows_per_chunk = LANES // GRID_W
    for c in range(SEQ // LANES):
        vt = v_ref[0, c * LANES:(c + 1) * LANES, :].astype(F32).T
        for h in range(2):
            vt_scr[h, c] = _head_rows_or_ones(vt, h)

    def key_row0(m):
        return _clip(NA_QROWS * m - NA_WIN_R // 2, 0, ROWS - NA_KROWS)

    def q_heads(m):
        q = q_ref[0, _rows(m, NA_QBLK), :]
        return [jnp.where(head_lanes[h], q, jnp.zeros_like(q)) for h in range(2)]

    def k_tile(m, t):
        return k_ref[0, _rows(key_row0(m) + t * (NA_KTILE // GRID_W), GRID_W, NA_KTILE), :]

    def vt_tile(m, t, h):
        c0 = key_row0(m) // rows_per_chunk + t * (NA_KTILE // LANES)
        return jnp.concatenate([vt_scr[h, c0 + c] for c in range(NA_KTILE // LANES)], axis=1)

    def bias_tile(m, t, h):
        if isinstance(m, int):
            cls = 0 if m == 0 else (2 if m == NA_BLOCKS - 1 else 1)
        else:
            cls = jnp.where(m == 0, 0, jnp.where(m == NA_BLOCKS - 1, 2, 1))
        return bias_ref[cls, h, t * NA_KTILE:(t + 1) * NA_KTILE, :]

    def write(m, out):
        o_ref[0, _rows(m, NA_QBLK), :] = out.astype(BF16)

    _pair_attention(NA_BLOCKS, NA_KBLK // NA_KTILE, NA_QBLK, NA_KTILE, q_heads, k_tile, vt_tile, bias_tile,
                    write, (s_a, s_b), (m_a, m_b))


def _nbr_attn(qkv, bias):
    B = qkv.shape[0]
    pairs = NA_HEADS // 2
    col = lambda off: pl.BlockSpec((1, SEQ, LANES), lambda p, b: (b, 0, off + p))
    return pl.pallas_call(
        _nbr_attn_kernel,
        grid=(pairs, B),
        in_specs=[col(0), col(pairs), col(2 * pairs),
                  pl.BlockSpec((3, 2, NA_KBLK, NA_QBLK), lambda p, b: (0, p, 0, 0))],
        out_specs=col(0),
        out_shape=jax.ShapeDtypeStruct((B, SEQ, NA_WIDTH), BF16),
        scratch_shapes=[pltpu.VMEM((2, NA_KBLK, NA_QBLK), F32)] * 2 + [pltpu.VMEM((2, 8, NA_QBLK), F32)] * 2
                       + [pltpu.VMEM((2, SEQ // LANES, LANES, LANES), BF16)],
        compiler_params=_params(2),
        name="nbr_attn",
    )(qkv, qkv, qkv, bias)


def _out_proj_ln(x, mixer_lo, mixer_hi, w_ref, g_ref, b_ref):
    half = w_ref.shape[0] // 2
    m = _dot(mixer_lo, w_ref[:half, :]) + _dot(mixer_hi, w_ref[half:, :])
    return _layernorm(ALPHA * x + m, g_ref[...], b_ref[...])


def _even_out_kernel(a_ref, u_ref, v_ref, x_ref, sw_ref, sb_ref, w_ref, g_ref, b_ref, o_ref, gate_scr):
    for c in range(TM // SGU_CHUNK):
        rows = slice(c * SGU_CHUNK, (c + 1) * SGU_CHUNK)
        for g in range(SGU_GROUPS):
            cols = slice(g * LANES, (g + 1) * LANES)
            mixed = _dot(sw_ref[g], v_ref[rows, cols]) + sb_ref[g]
            gate_scr[rows, cols] = (u_ref[rows, cols] * mixed).astype(BF16)
    o_ref[...] = _out_proj_ln(x_ref[...], a_ref[...], gate_scr[...], w_ref, g_ref, b_ref)


def _even_out(a, u, v, x, sgu_w, sgu_bias, w_out, ln_g, ln_b):
    T = x.shape[0]
    tok = lambda n: pl.BlockSpec((TM, n), lambda i: (i, 0))
    consts = (sgu_w, sgu_bias, w_out, ln_g, ln_b)
    return pl.pallas_call(
        _even_out_kernel,
        grid=(T // TM,),
        in_specs=[tok(NA_WIDTH), tok(SGU_WIDTH), tok(SGU_WIDTH), tok(D_MODEL)]
                 + [_const_spec(c.shape) for c in consts],
        out_specs=tok(D_MODEL),
        out_shape=jax.ShapeDtypeStruct((T, D_MODEL), F32),
        scratch_shapes=[pltpu.VMEM((TM, SGU_WIDTH), BF16)],
        compiler_params=_params(),
        name="even_out",
    )(a, u, v, x, *consts)


def _odd_in_kernel(x_ref, w_ref, qg_ref, wq_ref, kg_ref, wkv_ref, csq_ref, csk_ref,
                   q_ref, kn_ref, kr_ref, v_ref, d_ref):
    xb = x_ref[...].astype(BF16)
    h = _dot(xb, w_ref[...])
    o1 = MLA_Q_LORA
    o2 = o1 + MLA_KV_LORA
    o3 = o2 + LANES
    o4 = o3 + CONV_WIDTH

    cq = _rmsnorm(h[:, :o1], qg_ref[...]).astype(BF16)
    qf = _dot(cq, wq_ref[...])
    csq = csq_ref[...]
    for p in range(MLA_HEADS // 2):
        base = 2 * LANES * p
        q_ref[:, base:base + LANES] = qf[:, base:base + LANES].astype(BF16)
        prod = qf[:, base + LANES:base + 2 * LANES] * csq
        q_ref[:, base + LANES:base + 2 * LANES] = (prod + pltpu.roll(prod, 64, 1)).astype(BF16)

    ckv = _rmsnorm(h[:, o1:o2], kg_ref[...]).astype(BF16)
    kv = _dot(ckv, wkv_ref[...])
    kn_ref[...] = kv[:, :MLA_WIDTH].astype(BF16)
    v_ref[...] = kv[:, MLA_WIDTH:].astype(BF16)

    prod = h[:, o2:o3] * csk_ref[...]
    roped = prod + pltpu.roll(prod, LANES - MLA_ROPE, 1)
    lane = lax.broadcasted_iota(jnp.int32, (1, LANES), 1)
    kr_ref[...] = jnp.where(lane < MLA_ROPE, roped, pltpu.roll(roped, MLA_ROPE, 1)).astype(BF16)

    d_ref[...] = h[:, o3:o4] * _sigmoid(h[:, o4:])


def _odd_in(x, w, q_g, wq, kv_g, wkv, csq, csk):
    T = x.shape[0]
    tok = lambda n: pl.BlockSpec((TM, n), lambda i: (i, 0))
    pos = pl.BlockSpec((TM, LANES), lambda i: (i % TILES_PER_SEQ, 0))
    return pl.pallas_call(
        _odd_in_kernel,
        grid=(T // TM,),
        in_specs=[tok(D_MODEL), _const_spec(w.shape), _const_spec(q_g.shape), _const_spec(wq.shape),
                  _const_spec(kv_g.shape), _const_spec(wkv.shape), pos, pos],
        out_specs=[tok(2 * MLA_WIDTH), tok(MLA_WIDTH), tok(LANES), tok(MLA_WIDTH), tok(CONV_WIDTH)],
        out_shape=[jax.ShapeDtypeStruct((T, 2 * MLA_WIDTH), BF16),
                   jax.ShapeDtypeStruct((T, MLA_WIDTH), BF16),
                   jax.ShapeDtypeStruct((T, LANES), BF16),
                   jax.ShapeDtypeStruct((T, MLA_WIDTH), BF16),
                   jax.ShapeDtypeStruct((T, CONV_WIDTH), F32)],
        compiler_params=_params(),
        name="odd_in",
    )(x, w, q_g, wq, kv_g, wkv, csq, csk)


def _latent_attn_kernel(q_ref, kn_ref, kr_ref, v_ref, o_ref, k_scr, v_scr, s_a, s_b, m_a, m_b):
    lane = lax.broadcasted_iota(jnp.int32, (1, LANES), 1)
    lane2 = lax.broadcasted_iota(jnp.int32, (1, 2 * LANES), 1)
    k_scr[:, :LANES] = kn_ref[0]
    k_scr[:, LANES:] = kr_ref[0]
    v = v_ref[0]
    v_lanes = (lane < MLA_V, lane >= MLA_V)
    q_lanes = []
    for h in range(2):
        v_scr[h] = jnp.where(v_lanes[h], v, jnp.ones_like(v))
        nope = (lane2 >= h * MLA_NOPE) & (lane2 < (h + 1) * MLA_NOPE)
        rope = (lane2 >= LANES + h * MLA_ROPE) & (lane2 < LANES + (h + 1) * MLA_ROPE)
        q_lanes.append(nope | rope)

    def q_heads(i):
        q = q_ref[0, _rows(i, MLA_QBLK), :]
        return [jnp.where(q_lanes[h], q, jnp.zeros_like(q)) for h in range(2)]

    def k_tile(i, t):
        return k_scr[t * MLA_KTILE:(t + 1) * MLA_KTILE, :]

    def v_tile(i, t, h):
        return v_scr[h, t * MLA_KTILE:(t + 1) * MLA_KTILE, :]

    def write(i, out):
        o_ref[0, _rows(i, MLA_QBLK), :] = out.astype(BF16)

    _pair_attention_rows(SEQ // MLA_QBLK, SEQ // MLA_KTILE, MLA_QBLK, MLA_KTILE, q_heads, k_tile, v_tile,
                         write, (s_a, s_b), (m_a, m_b))


def _latent_attn(q, kn, kr, v):
    B = q.shape[0]
    pairs = MLA_HEADS // 2
    col = lambda width: pl.BlockSpec((1, SEQ, width), lambda b, p: (b, 0, p))
    return pl.pallas_call(
        _latent_attn_kernel,
        grid=(B, pairs),
        in_specs=[col(2 * LANES), col(LANES), pl.BlockSpec((1, SEQ, LANES), lambda b, p: (b, 0, 0)), col(LANES)],
        out_specs=col(LANES),
        out_shape=jax.ShapeDtypeStruct((B, SEQ, MLA_WIDTH), BF16),
        scratch_shapes=[pltpu.VMEM((SEQ, 2 * LANES), BF16), pltpu.VMEM((2, SEQ, LANES), BF16)]
                       + [pltpu.VMEM((2, MLA_QBLK, SEQ), F32)] * 2 + [pltpu.VMEM((2, MLA_QBLK, LANES), F32)] * 2,
        compiler_params=_params(2),
        name="latent_attn",
    )(q, kn, kr, v)


def _fill_halo_buffer(buf, prev_ref, cur_ref, next_ref, dtype):
    tm = cur_ref.shape[0]
    tiles_per_seq = SEQ // tm
    i = pl.program_id(0) % tiles_per_seq
    prev = prev_ref[...]
    nxt = next_ref[...]
    buf[:HALO, :] = jnp.where(i == 0, jnp.zeros_like(prev), prev).astype(dtype)
    buf[HALO:HALO + tm, :] = cur_ref[...].astype(dtype)
    buf[HALO + tm:, :] = jnp.where(i == tiles_per_seq - 1, jnp.zeros_like(nxt), nxt).astype(dtype)


def _halo_specs(T, tm, n):
    per = tm // HALO
    last = T // HALO - 1
    return (pl.BlockSpec((HALO, n), lambda i: (jnp.maximum(i * per - 1, 0), 0)),
            pl.BlockSpec((HALO, n), lambda i: (jnp.minimum((i + 1) * per, last), 0)))


def _odd_out_kernel(c_ref, dp_ref, d_ref, dn_ref, x_ref, cw_ref, cb_ref, cg_ref, cbeta_ref,
                    w_ref, g_ref, b_ref, o_ref, buf, conv_scr, shift_scr):
    _fill_halo_buffer(buf, dp_ref, d_ref, dn_ref, F32)
    first = HALO - CONV_K // 2
    sub = 8
    reach = (first + CONV_K - 1) // sub * sub
    for phase in range(1, sub):
        shift_scr[phase - 1] = buf[phase:phase + TM_CONV + reach, :]
    for c in range(CONV_WIDTH // LANES):
        cols = slice(c * LANES, (c + 1) * LANES)
        for r0 in range(0, TM_CONV, CONV_ROWS):
            acc = jnp.zeros((CONV_ROWS, LANES), F32) + cb_ref[:, cols]
            for k in range(CONV_K):
                phase = (k + first) % sub
                rows = slice(r0 + k + first - phase, r0 + k + first - phase + CONV_ROWS)
                tap = buf[rows, cols] if phase == 0 else shift_scr[phase - 1, rows, cols]
                acc = acc + tap * cw_ref[k:k + 1, cols]
            conv_scr[r0:r0 + CONV_ROWS, cols] = acc
    y = _layernorm(conv_scr[...], cg_ref[...], cbeta_ref[...])
    y = (y * _sigmoid(y)).astype(BF16)
    o_ref[...] = _out_proj_ln(x_ref[...], c_ref[...], y, w_ref, g_ref, b_ref)


def _odd_out(c, d, x, conv_w, conv_b, conv_g, conv_beta, w_out, ln_g, ln_b):
    T = x.shape[0]
    tok = lambda n: pl.BlockSpec((TM_CONV, n), lambda i: (i, 0))
    prev, nxt = _halo_specs(T, TM_CONV, CONV_WIDTH)
    consts = (conv_w, conv_b, conv_g, conv_beta, w_out, ln_g, ln_b)
    return pl.pallas_call(
        _odd_out_kernel,
        grid=(T // TM_CONV,),
        in_specs=[tok(MLA_WIDTH), prev, tok(CONV_WIDTH), nxt, tok(D_MODEL)]
                 + [_const_spec(a.shape) for a in consts],
        out_specs=tok(D_MODEL),
        out_shape=jax.ShapeDtypeStruct((T, D_MODEL), F32),
        scratch_shapes=[pltpu.VMEM((TM_CONV + 2 * HALO, CONV_WIDTH), F32), pltpu.VMEM((TM_CONV, CONV_WIDTH), F32),
                        pltpu.VMEM((7, TM_CONV + HALO + 8, CONV_WIDTH), F32)],
        compiler_params=_params(),
        name="odd_out",
    )(c, d, d, d, x, *consts)


def _conv_ffn_kernel(xp_ref, x_ref, xn_ref, wu_ref, cw_ref, cb_ref, wd_ref, g_ref, b_ref, o_ref,
                     lhs, h_a, h_b, acc):
    _fill_halo_buffer(lhs, xp_ref, x_ref, xn_ref, BF16)

    n_blocks = TM_FFN // FF_ROWS
    margin = 8
    total = TM_FFN + 2 * HALO
    up_rows = [0] + [HALO + b * FF_ROWS for b in range(1, n_blocks)] + [total]

    def up(j, blk, h_scr):
        rows = slice(up_rows[blk], up_rows[blk + 1])
        h_scr[rows, :] = _dot(lhs[rows, :], wu_ref[j])

    def down(j, blk, h_scr, first=False):
        cw = cw_ref[j]
        lo = HALO + blk * FF_ROWS - margin
        n = FF_ROWS + 2 * margin
        h = h_scr[lo:lo + n, :]
        conv = (pltpu.roll(h, 1, 0) * cw[0:1, :] + h * cw[1:2, :] + pltpu.roll(h, n - 1, 0) * cw[2:3, :]
                + cb_ref[j])[margin:margin + FF_ROWS, :]
        gate = conv[:, :FF_CHUNK]
        act = (gate * (1.0 / (1.0 + jnp.exp2(gate))) * conv[:, FF_CHUNK:]).astype(BF16)
        part = _dot(act, wd_ref[j])
        rows = slice(blk * FF_ROWS, (blk + 1) * FF_ROWS)
        if first:
            acc[rows, :] = part
        else:
            acc[rows, :] += part

    def step(j, cur, nxt, first=False):
        for blk in range(n_blocks):
            if nxt is not None:
                up(j + 1, blk, nxt)
            down(j, blk, cur, first)

    for blk in range(n_blocks):
        up(0, blk, h_a)
    step(0, h_a, h_b, first=True)

    def pair(i, carry):
        j = 2 * i + 1
        step(j, h_b, h_a)
        step(j + 1, h_a, h_b)
        return carry

    assert N_FF_CHUNKS % 2 == 1 and N_FF_CHUNKS >= 3
    lax.fori_loop(0, (N_FF_CHUNKS - 3) // 2, pair, 0)
    step(N_FF_CHUNKS - 2, h_b, h_a)
    step(N_FF_CHUNKS - 1, h_a, None)
    o_ref[...] = _layernorm(ALPHA * x_ref[...] + acc[...], g_ref[...], b_ref[...])


def _conv_ffn(x, w_up, conv_w, conv_b, w_down, ln_g, ln_b):
    T = x.shape[0]
    tok = pl.BlockSpec((TM_FFN, D_MODEL), lambda i: (i, 0))
    prev, nxt = _halo_specs(T, TM_FFN, D_MODEL)
    consts = (w_up, conv_w, conv_b, w_down, ln_g, ln_b)
    h_buf = pltpu.VMEM((TM_FFN + 2 * HALO, 2 * FF_CHUNK), F32)
    return pl.pallas_call(
        _conv_ffn_kernel,
        grid=(T // TM_FFN,),
        in_specs=[prev, tok, nxt] + [_const_spec(a.shape) for a in consts],
        out_specs=tok,
        out_shape=jax.ShapeDtypeStruct((T, D_MODEL), F32),
        scratch_shapes=[pltpu.VMEM((TM_FFN + 2 * HALO, D_MODEL), BF16), h_buf, h_buf,
                        pltpu.VMEM((TM_FFN, D_MODEL), F32)],
        compiler_params=_params(),
        name="conv_ffn",
    )(x, x, x, *consts)


def _row(v):
    return v.reshape(1, -1).astype(F32)


def _nbr_bias(rpb):
    c = np.arange(GRID_W)
    c0 = np.clip(c - NA_WIN_C // 2, 0, GRID_W - NA_WIN_C)
    col_ok = (c[None, :] >= c0[:, None]) & (c[None, :] < c0[:, None] + NA_WIN_C)
    dc = np.clip(c[None, :] - c[:, None], -(NA_WIN_C - 1), NA_WIN_C - 1) + NA_WIN_C - 1
    row_tab = jnp.where(col_ok[None, None], rpb[:, :, dc] * LOG2E, NEG)
    row_tab = jnp.concatenate([row_tab, jnp.full_like(row_tab[:, :1], NEG)], axis=1)
    masked = 2 * NA_WIN_R - 1
    sel = np.full((3, NA_QROWS, NA_KROWS), masked)
    for cls, m in enumerate((0, 1, NA_BLOCKS - 1)):
        key_row0 = min(max(NA_QROWS * m - NA_WIN_R // 2, 0), ROWS - NA_KROWS)
        for i in range(NA_QROWS):
            qr = NA_QROWS * m + i
            r0 = min(max(qr - NA_WIN_R // 2, 0), ROWS - NA_WIN_R)
            for j in range(NA_KROWS):
                kr = key_row0 + j
                if r0 <= kr < r0 + NA_WIN_R:
                    sel[cls, i, j] = kr - qr + NA_WIN_R - 1
    b = row_tab[:, sel]
    b = b.transpose(1, 0, 3, 5, 2, 4)
    return b.reshape(3, NA_HEADS, NA_KBLK, NA_QBLK).astype(F32)


def _rope_tables():
    pos = jnp.arange(SEQ, dtype=F32)
    inv = ROPE_THETA ** (-jnp.arange(0, MLA_ROPE, 2, dtype=F32) / MLA_ROPE)
    ang = pos[:, None] * inv[None, :]
    cos, sin = jnp.cos(ang), jnp.sin(ang)
    csq = jnp.concatenate([jnp.tile(cos, (1, 4)), jnp.tile(sin, (1, 4))], axis=1)
    csk = jnp.concatenate([jnp.tile(cos, (1, 2)), jnp.tile(sin, (1, 2)),
                           jnp.zeros((SEQ, LANES - 2 * MLA_ROPE), F32)], axis=1)
    return csq, csk


def _rot_cols(w):
    half = w.shape[-1] // 2
    return jnp.concatenate([-w[..., half:], w[..., :half]], axis=-1)


def _odd_weights(w_in, w_uq, w_ukv):
    o1 = MLA_Q_LORA
    o2 = o1 + MLA_KV_LORA
    o3 = o2 + MLA_ROPE
    k_r = w_in[:, o2:o3]
    k_blk = jnp.concatenate([k_r, _rot_cols(k_r), jnp.zeros((D_MODEL, LANES - 2 * MLA_ROPE), F32)], axis=1)
    w_in2 = jnp.concatenate([w_in[:, :o2], k_blk, w_in[:, o3:]], axis=1)

    scale = (MLA_NOPE + MLA_ROPE) ** -0.5 * LOG2E
    wq =(w_uq * scale).reshape(MLA_Q_LORA, MLA_HEADS, MLA_NOPE + MLA_ROPE)
    nope, rope = wq[:, :, :MLA_NOPE], wq[:, :, MLA_NOPE:]
    rot = _rot_cols(rope)
    blocks = []
    for p in range(MLA_HEADS // 2):
        h0, h1 = 2 * p, 2 * p + 1
        blocks += [nope[:, h0], nope[:, h1], rope[:, h0], rope[:, h1], rot[:, h0], rot[:, h1]]
    wq2 = jnp.concatenate(blocks, axis=1)

    wkv = w_ukv.reshape(MLA_KV_LORA, MLA_HEADS, MLA_NOPE + MLA_V)
    wkv2 = jnp.concatenate([wkv[:, :, :MLA_NOPE].reshape(MLA_KV_LORA, -1),
                            wkv[:, :, MLA_NOPE:].reshape(MLA_KV_LORA, -1)], axis=1)
    return w_in2.astype(BF16), wq2.astype(BF16), wkv2.astype(BF16)


def _ffn_weights(w_up, conv_w, conv_b, w_down):
    def chunked(a):
        g = a[..., :D_FF].reshape(a.shape[:-1] + (N_FF_CHUNKS, FF_CHUNK))
        v = a[..., D_FF:].reshape(a.shape[:-1] + (N_FF_CHUNKS, FF_CHUNK))
        return jnp.moveaxis(jnp.concatenate([g, v], axis=-1), -2, 0)
    col_scale = jnp.concatenate([jnp.full((D_FF,), -LOG2E, F32), jnp.full((D_FF,), -1.0 / LOG2E, F32)])
    return (chunked(w_up).astype(BF16), chunked(conv_w * col_scale).astype(F32),
            chunked((conv_b * col_scale)[None, :]).astype(F32),
            w_down.reshape(N_FF_CHUNKS, FF_CHUNK, D_MODEL).astype(BF16))


def _trunk(xs, w_in_even, rpb, sgu_ln_g, sgu_ln_b, sgu_w, sgu_b, w_out_even,
           w_in_odd, q_norm_g, w_uq, kv_norm_g, w_ukv, conv_w, conv_b, conv_ln_g, conv_ln_b,
           w_out_odd, ffn_w_up, ffn_conv_w, ffn_conv_b, ffn_w_down, ln1_g, ln1_b, ln2_g, ln2_b):
    shapes = [x.shape for x in xs]
    xs = [x.reshape(-1, D_MODEL) for x in xs]
    per_seq = lambda a: a.reshape(-1, SEQ, a.shape[-1])
    csq, csk = _rope_tables()
    for l in range(DEPTH):
        i = l // 2
        ln1 = (_row(ln1_g[l]), _row(ln1_b[l]))
        if l % 2 == 0:
            w_in = jnp.concatenate([w_in_even[i][:, :NA_WIDTH] * (NA_HEAD_DIM ** -0.5 * LOG2E),
                                    w_in_even[i][:, NA_WIDTH:]], axis=1).astype(BF16)
            sgu_ln = (_row(sgu_ln_g[i]), _row(sgu_ln_b[i]))
            bias = _nbr_bias(rpb[i])
            sgu_bias = jnp.broadcast_to(sgu_b[i][:, :, None], (SGU_GROUPS, SGU_CHUNK, LANES)).astype(F32)
            sw, w_out = sgu_w[i].astype(BF16), w_out_even[i].astype(BF16)

            def mixer(x):
                qkv, u, v = _even_in(x, w_in, *sgu_ln)
                a = _nbr_attn(per_seq(qkv), bias).reshape(-1, NA_WIDTH)
                return _even_out(a, u, v, x, sw, sgu_bias, w_out, *ln1)
        else:
            w_in, wq, wkv = _odd_weights(w_in_odd[i], w_uq[i], w_ukv[i])
            norms = (_row(q_norm_g[i]), _row(kv_norm_g[i]))
            cw = jnp.concatenate([conv_w[i], jnp.zeros((1, CONV_WIDTH), F32)], axis=0)
            conv = (cw, _row(conv_b[i]), _row(conv_ln_g[i]), _row(conv_ln_b[i]))
            w_out = w_out_odd[i].astype(BF16)

            def mixer(x):
                q, kn, kr, v, d = _odd_in(x, w_in, norms[0], wq, norms[1], wkv, csq, csk)
                c = _latent_attn(per_seq(q), per_seq(kn), per_seq(kr), per_seq(v)).reshape(-1, MLA_WIDTH)
                return _odd_out(c, d, x, *conv, w_out, *ln1)
        ffn = _ffn_weights(ffn_w_up[l], ffn_conv_w[l], ffn_conv_b[l], ffn_w_down[l])
        xs = [_conv_ffn(mixer(x), *ffn, _row(ln2_g[l]), _row(ln2_b[l])) for x in xs]
    return tuple(x.reshape(shape) for x, shape in zip(xs, shapes))


def kernel(x_prompt, x_sample, w_in_even, rpb, sgu_ln_g, sgu_ln_b, sgu_w, sgu_b, w_out_even, w_in_odd, q_norm_g, w_uq, kv_norm_g, w_ukv, conv_w, conv_b, conv_ln_g, conv_ln_b, w_out_odd, ffn_w_up, ffn_conv_w, ffn_conv_b, ffn_w_down, ln1_g, ln1_b, ln2_g, ln2_b):
    return _trunk([x_prompt, x_sample], w_in_even, rpb, sgu_ln_g, sgu_ln_b, sgu_w, sgu_b, w_out_even,
                  w_in_odd, q_norm_g, w_uq, kv_norm_g, w_ukv, conv_w, conv_b, conv_ln_g, conv_ln_b,
                  w_out_odd, ffn_w_up, ffn_conv_w, ffn_conv_b, ffn_w_down, ln1_g, ln1_b, ln2_g, ln2_b)
```

```python
import functools

import numpy as np
import jax
import jax.numpy as jnp
from jax import lax
from jax.experimental import pallas as pl
from jax.experimental.pallas import tpu as pltpu

F32 = jnp.float32
BF16 = jnp.bfloat16

D_MODEL = 1024
SEQ = 2048
DEPTH = 4
GRID_W = 64
ROWS = SEQ // GRID_W
NA_HEADS = 8
NA_HEAD_DIM = 64
NA_WIDTH = 512
NA_WIN_R = 8
NA_WIN_C = 16
SGU_WIDTH = 512
SGU_GROUPS = 4
SGU_CHUNK = 128
MLA_HEADS = 8
MLA_Q_LORA = 256
MLA_KV_LORA = 128
MLA_NOPE = 64
MLA_ROPE = 32
MLA_V = 64
MLA_WIDTH = 512
ROPE_THETA = 10000.0
CONV_WIDTH = 512
CONV_K = 31
D_FF = 2816
LN_EPS = 1e-5
RMS_EPS = 1e-6
ALPHA = (2 * DEPTH) ** 0.25

VMEM_LIMIT_BYTES = 56 * 1024 * 1024
LANES = 128
HALO = 16
TM = 1024
TM_CONV = 512
TILES_PER_SEQ = SEQ // TM
TM_FFN = 1024
CONV_ROWS = 128
FF_ROWS = 512
FF_CHUNK = 256
N_FF_CHUNKS = D_FF // FF_CHUNK
NA_QROWS = 4
NA_KROWS = 12
NA_QBLK = NA_QROWS * GRID_W
NA_KBLK = NA_KROWS * GRID_W
NA_BLOCKS = ROWS // NA_QROWS
MLA_QBLK = 512
MLA_KTILE = 512
NA_KTILE = 256
LOG2E = 1.4426950408889634
NEG = -1e30


def _params(n_axes=1):
    return pltpu.CompilerParams(dimension_semantics=("arbitrary",) * n_axes,
                                vmem_limit_bytes=VMEM_LIMIT_BYTES)


def _const_spec(shape):
    nd = len(shape)
    return pl.BlockSpec(shape, lambda *_: (0,) * nd, pipeline_mode=pl.Buffered(1))


def _layernorm(z, g, b):
    mu = jnp.mean(z, axis=-1, keepdims=True)
    zc = z - mu
    var = jnp.mean(zc * zc, axis=-1, keepdims=True)
    return zc * lax.rsqrt(var + LN_EPS) * g + b


def _rmsnorm(z, g):
    return z * lax.rsqrt(jnp.mean(z * z, axis=-1, keepdims=True) + RMS_EPS) * g


def _sigmoid(z):
    return 1.0 / (1.0 + jnp.exp(-z))


def _dot(a, b):
    return jnp.dot(a, b, preferred_element_type=F32)


def _dot_nt(a, b):
    return lax.dot_general(a, b, (((1,), (1,)), ((), ())), preferred_element_type=F32)


def _even_in_kernel(x_ref, w_ref, g_ref, b_ref, qkv_ref, u_ref, v_ref):
    xb = x_ref[...].astype(BF16)
    qkv_ref[...] = _dot(xb, w_ref[:, :3 * NA_WIDTH]).astype(BF16)
    ug = _dot(xb, w_ref[:, 3 * NA_WIDTH:])
    u_ref[...] = jax.nn.gelu(ug[:, :SGU_WIDTH])
    v = _layernorm(jax.nn.gelu(ug[:, SGU_WIDTH:]), g_ref[...], b_ref[...])
    v_ref[...] = v.astype(BF16)


def _even_in(x, w, ln_g, ln_b):
    T = x.shape[0]
    tok = lambda n: pl.BlockSpec((TM, n), lambda i: (i, 0))
    return pl.pallas_call(
        _even_in_kernel,
        grid=(T // TM,),
        in_specs=[tok(D_MODEL), _const_spec(w.shape), _const_spec(ln_g.shape), _const_spec(ln_b.shape)],
        out_specs=[tok(3 * NA_WIDTH), tok(SGU_WIDTH), tok(SGU_WIDTH)],
        out_shape=[jax.ShapeDtypeStruct((T, 3 * NA_WIDTH), BF16),
                   jax.ShapeDtypeStruct((T, SGU_WIDTH), F32),
                   jax.ShapeDtypeStruct((T, SGU_WIDTH), BF16)],
        compiler_params=_params(),
        name="even_in",
    )(x, w, ln_g, ln_b)


def _rows(i, size, n=None):
    n = size if n is None else n
    start = i * size
    if not isinstance(i, int):
        start = pl.multiple_of(start, size)
    return pl.ds(start, n)


def _clip(v, lo, hi):
    return min(max(v, lo), hi) if isinstance(v, int) else jnp.clip(v, lo, hi)


def _pair_attention(n_blocks, n_tiles, qb, kt, q_heads, k_tile, vt_tile, bias_tile, write, s_bufs, m_bufs):
    half = LANES // 2

    def run(prod, cons):
        if prod is not None:
            ip, s_p, m_p = prod
            qh = q_heads(ip)
            mx = [jnp.full((8, qb), NEG, F32) for _ in range(2)]
        if cons is not None:
            ic, s_c, m_c = cons
            m = [m_c[h, 0:1, :] for h in range(2)]
            acc = [jnp.zeros((LANES, qb), F32) for _ in range(2)]
        for t in range(n_tiles):
            keys = slice(t * kt, (t + 1) * kt)
            if prod is not None:
                k = k_tile(ip, t)
                for h in range(2):
                    s = _dot_nt(k, qh[h])
                    b = bias_tile(ip, t, h)
                    if b is not None:
                        s = s + b
                    s_p[h, keys, :] = s
                    mx[h] = jnp.maximum(mx[h], jnp.max(s.reshape(kt // 8, 8, qb), axis=0))
            if cons is not None:
                for h in range(2):
                    p = jnp.exp2(s_c[h, keys, :] - m[h]).astype(BF16)
                    acc[h] = acc[h] + _dot(vt_tile(ic, t, h), p)
        if prod is not None:
            for h in range(2):
                m_p[h] = jnp.broadcast_to(jnp.max(mx[h], axis=0, keepdims=True), (8, qb))
        if cons is not None:
            out_t = jnp.concatenate([acc[0][:half] * (1.0 / acc[0][half:]),
                                     acc[1][half:] * (1.0 / acc[1][:half])], axis=0)
            write(ic, out_t.T)

    _two_buffer_pipeline(n_blocks, run, s_bufs, m_bufs)


def _two_buffer_pipeline(n_blocks, run, s_bufs, m_bufs):
    assert n_blocks % 2 == 0 and n_blocks >= 2
    buf_a = (s_bufs[0], m_bufs[0])
    buf_b = (s_bufs[1], m_bufs[1])
    run((0,) + buf_a, None)

    def pair(u, carry):
        i = 2 * u
        run((i + 1,) + buf_b, (i,) + buf_a)
        run((i + 2,) + buf_a, (i + 1,) + buf_b)
        return carry

    lax.fori_loop(0, n_blocks // 2 - 1, pair, 0)
    run((n_blocks - 1,) + buf_b, (n_blocks - 2,) + buf_a)
    run(None, (n_blocks - 1,) + buf_b)


def _pair_attention_rows(n_blocks, n_tiles, qb, kt, q_heads, k_tile, v_tile, write, s_bufs, m_bufs):
    lane_chunks = kt // LANES

    def run(prod, cons):
        if prod is not None:
            ip, s_p, m_p = prod
            qh = q_heads(ip)
            mx = [jnp.full((qb, LANES), NEG, F32) for _ in range(2)]
        if cons is not None:
            ic, s_c, m_c = cons
            m = [m_c[h] for h in range(2)]
            acc = [jnp.zeros((qb, LANES), F32) for _ in range(2)]
        for t in range(n_tiles):
            if prod is not None:
                k = k_tile(ip, t)
                for h in range(2):
                    s = _dot_nt(qh[h], k)
                    s_p[h, :, t * kt:(t + 1) * kt] = s
                    for c in range(lane_chunks):
                        mx[h] = jnp.maximum(mx[h], s[:, c * LANES:(c + 1) * LANES])
            if cons is not None:
                for h in range(2):
                    p = [jnp.exp2(s_c[h, :, t * kt + c * LANES:t * kt + (c + 1) * LANES] - m[h])
                         for c in range(lane_chunks)]
                    p = jnp.concatenate(p, axis=1).astype(BF16)
                    acc[h] = acc[h] + _dot(p, v_tile(ic, t, h))
        if prod is not None:
            for h in range(2):
                m_p[h] = jnp.broadcast_to(jnp.max(mx[h], axis=-1, keepdims=True), (qb, LANES))
        if cons is not None:
            lane = lax.broadcasted_iota(jnp.int32, (1, LANES), 1)
            halves = [acc[h] * pltpu.roll(1.0 / acc[h], LANES // 2, 1) for h in range(2)]
            write(ic, jnp.where(lane < LANES // 2, halves[0], halves[1]))

    _two_buffer_pipeline(n_blocks, run, s_bufs, m_bufs)


def _head_rows_or_ones(vt, h):
    row = lax.broadcasted_iota(jnp.int32, (LANES, 1), 0)
    mine = (row < LANES // 2) if h == 0 else (row >= LANES // 2)
    return jnp.where(mine, vt, 1.0).astype(BF16)


def _nbr_attn_kernel(q_ref, k_ref, v_ref, bias_ref, o_ref, s_a, s_b, m_a, m_b, vt_scr):
    lane = lax.broadcasted_iota(jnp.int32, (1, LANES), 1)
    head_lanes = (lane < NA_HEAD_DIM, lane >= NA_HEAD_DIM)
    rows_per_chunk = LANES // GRID_W
    for c in range(SEQ // LANES):
        vt = v_ref[0, c * LANES:(c + 1) * LANES, :].astype(F32).T
        for h in range(2):
            vt_scr[h, c] = _head_rows_or_ones(vt, h)

    def key_row0(m):
        return _clip(NA_QROWS * m - NA_WIN_R // 2, 0, ROWS - NA_KROWS)

    def q_heads(m):
        q = q_ref[0, _rows(m, NA_QBLK), :]
        return [jnp.where(head_lanes[h], q, jnp.zeros_like(q)) for h in range(2)]

    def k_tile(m, t):
        return k_ref[0, _rows(key_row0(m) + t * (NA_KTILE // GRID_W), GRID_W, NA_KTILE), :]

    def vt_tile(m, t, h):
        c0 = key_row0(m) // rows_per_chunk + t * (NA_KTILE // LANES)
        return jnp.concatenate([vt_scr[h, c0 + c] for c in range(NA_KTILE // LANES)], axis=1)

    def bias_tile(m, t, h):
        if isinstance(m, int):
            cls = 0 if m == 0 else (2 if m == NA_BLOCKS - 1 else 1)
        else:
            cls = jnp.where(m == 0, 0, jnp.where(m == NA_BLOCKS - 1, 2, 1))
        return bias_ref[cls, h, t * NA_KTILE:(t + 1) * NA_KTILE, :]

    def write(m, out):
        o_ref[0, _rows(m, NA_QBLK), :] = out.astype(BF16)

    _pair_attention(NA_BLOCKS, NA_KBLK // NA_KTILE, NA_QBLK, NA_KTILE, q_heads, k_tile, vt_tile, bias_tile,
                    write, (s_a, s_b), (m_a, m_b))


def _nbr_attn(qkv, bias):
    B = qkv.shape[0]
    pairs = NA_HEADS // 2
    col = lambda off: pl.BlockSpec((1, SEQ, LANES), lambda p, b: (b, 0, off + p))
    return pl.pallas_call(
        _nbr_attn_kernel,
        grid=(pairs, B),
        in_specs=[col(0), col(pairs), col(2 * pairs),
                  pl.BlockSpec((3, 2, NA_KBLK, NA_QBLK), lambda p, b: (0, p, 0, 0))],
        out_specs=col(0),
        out_shape=jax.ShapeDtypeStruct((B, SEQ, NA_WIDTH), BF16),
        scratch_shapes=[pltpu.VMEM((2, NA_KBLK, NA_QBLK), F32)] * 2 + [pltpu.VMEM((2, 8, NA_QBLK), F32)] * 2
                       + [pltpu.VMEM((2, SEQ // LANES, LANES, LANES), BF16)],
        compiler_params=_params(2),
        name="nbr_attn",
    )(qkv, qkv, qkv, bias)


def _out_proj_ln(x, mixer_lo, mixer_hi, w_ref, g_ref, b_ref):
    half = w_ref.shape[0] // 2
    m = _dot(mixer_lo, w_ref[:half, :]) + _dot(mixer_hi, w_ref[half:, :])
    return _layernorm(ALPHA * x + m, g_ref[...], b_ref[...])


def _even_out_kernel(a_ref, u_ref, v_ref, x_ref, sw_ref, sb_ref, w_ref, g_ref, b_ref, o_ref, gate_scr):
    for c in range(TM // SGU_CHUNK):
        rows = slice(c * SGU_CHUNK, (c + 1) * SGU_CHUNK)
        for g in range(SGU_GROUPS):
            cols = slice(g * LANES, (g + 1) * LANES)
            mixed = _dot(sw_ref[g], v_ref[rows, cols]) + sb_ref[g]
            gate_scr[rows, cols] = (u_ref[rows, cols] * mixed).astype(BF16)
    o_ref[...] = _out_proj_ln(x_ref[...], a_ref[...], gate_scr[...], w_ref, g_ref, b_ref)


def _even_out(a, u, v, x, sgu_w, sgu_bias, w_out, ln_g, ln_b):
    T = x.shape[0]
    tok = lambda n: pl.BlockSpec((TM, n), lambda i: (i, 0))
    consts = (sgu_w, sgu_bias, w_out, ln_g, ln_b)
    return pl.pallas_call(
        _even_out_kernel,
        grid=(T // TM,),
        in_specs=[tok(NA_WIDTH), tok(SGU_WIDTH), tok(SGU_WIDTH), tok(D_MODEL)]
                 + [_const_spec(c.shape) for c in consts],
        out_specs=tok(D_MODEL),
        out_shape=jax.ShapeDtypeStruct((T, D_MODEL), F32),
        scratch_shapes=[pltpu.VMEM((TM, SGU_WIDTH), BF16)],
        compiler_params=_params(),
        name="even_out",
    )(a, u, v, x, *consts)


def _odd_in_kernel(x_ref, w_ref, qg_ref, wq_ref, kg_ref, wkv_ref, csq_ref, csk_ref,
                   q_ref, kn_ref, kr_ref, v_ref, d_ref):
    xb = x_ref[...].astype(BF16)
    h = _dot(xb, w_ref[...])
    o1 = MLA_Q_LORA
    o2 = o1 + MLA_KV_LORA
    o3 = o2 + LANES
    o4 = o3 + CONV_WIDTH

    cq = _rmsnorm(h[:, :o1], qg_ref[...]).astype(BF16)
    qf = _dot(cq, wq_ref[...])
    csq = csq_ref[...]
    for p in range(MLA_HEADS // 2):
        base = 2 * LANES * p
        q_ref[:, base:base + LANES] = qf[:, base:base + LANES].astype(BF16)
        prod = qf[:, base + LANES:base + 2 * LANES] * csq
        q_ref[:, base + LANES:base + 2 * LANES] = (prod + pltpu.roll(prod, 64, 1)).astype(BF16)

    ckv = _rmsnorm(h[:, o1:o2], kg_ref[...]).astype(BF16)
    kv = _dot(ckv, wkv_ref[...])
    kn_ref[...] = kv[:, :MLA_WIDTH].astype(BF16)
    v_ref[...] = kv[:, MLA_WIDTH:].astype(BF16)

    prod = h[:, o2:o3] * csk_ref[...]
    roped = prod + pltpu.roll(prod, LANES - MLA_ROPE, 1)
    lane = lax.broadcasted_iota(jnp.int32, (1, LANES), 1)
    kr_ref[...] = jnp.where(lane < MLA_ROPE, roped, pltpu.roll(roped, MLA_ROPE, 1)).astype(BF16)

    d_ref[...] = h[:, o3:o4] * _sigmoid(h[:, o4:])


def _odd_in(x, w, q_g, wq, kv_g, wkv, csq, csk):
    T = x.shape[0]
    tok = lambda n: pl.BlockSpec((TM, n), lambda i: (i, 0))
    pos = pl.BlockSpec((TM, LANES), lambda i: (i % TILES_PER_SEQ, 0))
    return pl.pallas_call(
        _odd_in_kernel,
        grid=(T // TM,),
        in_specs=[tok(D_MODEL), _const_spec(w.shape), _const_spec(q_g.shape), _const_spec(wq.shape),
                  _const_spec(kv_g.shape), _const_spec(wkv.shape), pos, pos],
        out_specs=[tok(2 * MLA_WIDTH), tok(MLA_WIDTH), tok(LANES), tok(MLA_WIDTH), tok(CONV_WIDTH)],
        out_shape=[jax.ShapeDtypeStruct((T, 2 * MLA_WIDTH), BF16),
                   jax.ShapeDtypeStruct((T, MLA_WIDTH), BF16),
                   jax.ShapeDtypeStruct((T, LANES), BF16),
                   jax.ShapeDtypeStruct((T, MLA_WIDTH), BF16),
                   jax.ShapeDtypeStruct((T, CONV_WIDTH), F32)],
        compiler_params=_params(),
        name="odd_in",
    )(x, w, q_g, wq, kv_g, wkv, csq, csk)


def _latent_attn_kernel(q_ref, kn_ref, kr_ref, v_ref, o_ref, k_scr, v_scr, s_a, s_b, m_a, m_b):
    lane = lax.broadcasted_iota(jnp.int32, (1, LANES), 1)
    lane2 = lax.broadcasted_iota(jnp.int32, (1, 2 * LANES), 1)
    k_scr[:, :LANES] = kn_ref[0]
    k_scr[:, LANES:] = kr_ref[0]
    v = v_ref[0]
    v_lanes = (lane < MLA_V, lane >= MLA_V)
    q_lanes = []
    for h in range(2):
        v_scr[h] = jnp.where(v_lanes[h], v, jnp.ones_like(v))
        nope = (lane2 >= h * MLA_NOPE) & (lane2 < (h + 1) * MLA_NOPE)
        rope = (lane2 >= LANES + h * MLA_ROPE) & (lane2 < LANES + (h + 1) * MLA_ROPE)
        q_lanes.append(nope | rope)

    def q_heads(i):
        q = q_ref[0, _rows(i, MLA_QBLK), :]
        return [jnp.where(q_lanes[h], q, jnp.zeros_like(q)) for h in range(2)]

    def k_tile(i, t):
        return k_scr[t * MLA_KTILE:(t + 1) * MLA_KTILE, :]

    def v_tile(i, t, h):
        return v_scr[h, t * MLA_KTILE:(t + 1) * MLA_KTILE, :]

    def write(i, out):
        o_ref[0, _rows(i, MLA_QBLK), :] = out.astype(BF16)

    _pair_attention_rows(SEQ // MLA_QBLK, SEQ // MLA_KTILE, MLA_QBLK, MLA_KTILE, q_heads, k_tile, v_tile,
                         write, (s_a, s_b), (m_a, m_b))


def _latent_attn(q, kn, kr, v):
    B = q.shape[0]
    pairs = MLA_HEADS // 2
    col = lambda width: pl.BlockSpec((1, SEQ, width), lambda b, p: (b, 0, p))
    return pl.pallas_call(
        _latent_attn_kernel,
        grid=(B, pairs),
        in_specs=[col(2 * LANES), col(LANES), pl.BlockSpec((1, SEQ, LANES), lambda b, p: (b, 0, 0)), col(LANES)],
        out_specs=col(LANES),
        out_shape=jax.ShapeDtypeStruct((B, SEQ, MLA_WIDTH), BF16),
        scratch_shapes=[pltpu.VMEM((SEQ, 2 * LANES), BF16), pltpu.VMEM((2, SEQ, LANES), BF16)]
                       + [pltpu.VMEM((2, MLA_QBLK, SEQ), F32)] * 2 + [pltpu.VMEM((2, MLA_QBLK, LANES), F32)] * 2,
        compiler_params=_params(2),
        name="latent_attn",
    )(q, kn, kr, v)


def _fill_halo_buffer(buf, prev_ref, cur_ref, next_ref, dtype):
    tm = cur_ref.shape[0]
    tiles_per_seq = SEQ // tm
    i = pl.program_id(0) % tiles_per_seq
    prev = prev_ref[...]
    nxt = next_ref[...]
    buf[:HALO, :] = jnp.where(i == 0, jnp.zeros_like(prev), prev).astype(dtype)
    buf[HALO:HALO + tm, :] = cur_ref[...].astype(dtype)
    buf[HALO + tm:, :] = jnp.where(i == tiles_per_seq - 1, jnp.zeros_like(nxt), nxt).astype(dtype)


def _halo_specs(T, tm, n):
    per = tm // HALO
    last = T // HALO - 1
    return (pl.BlockSpec((HALO, n), lambda i: (jnp.maximum(i * per - 1, 0), 0)),
            pl.BlockSpec((HALO, n), lambda i: (jnp.minimum((i + 1) * per, last), 0)))


def _odd_out_kernel(c_ref, dp_ref, d_ref, dn_ref, x_ref, cw_ref, cb_ref, cg_ref, cbeta_ref,
                    w_ref, g_ref, b_ref, o_ref, buf, conv_scr, shift_scr):
    _fill_halo_buffer(buf, dp_ref, d_ref, dn_ref, F32)
    first = HALO - CONV_K // 2
    sub = 8
    reach = (first + CONV_K - 1) // sub * sub
    for phase in range(1, sub):
        shift_scr[phase - 1] = buf[phase:phase + TM_CONV + reach, :]
    for c in range(CONV_WIDTH // LANES):
        cols = slice(c * LANES, (c + 1) * LANES)
        for r0 in range(0, TM_CONV, CONV_ROWS):
            acc = jnp.zeros((CONV_ROWS, LANES), F32) + cb_ref[:, cols]
            for k in range(CONV_K):
                phase = (k + first) % sub
                rows = slice(r0 + k + first - phase, r0 + k + first - phase + CONV_ROWS)
                tap = buf[rows, cols] if phase == 0 else shift_scr[phase - 1, rows, cols]
                acc = acc + tap * cw_ref[k:k + 1, cols]
            conv_scr[r0:r0 + CONV_ROWS, cols] = acc
    y = _layernorm(conv_scr[...], cg_ref[...], cbeta_ref[...])
    y = (y * _sigmoid(y)).astype(BF16)
    o_ref[...] = _out_proj_ln(x_ref[...], c_ref[...], y, w_ref, g_ref, b_ref)


def _odd_out(c, d, x, conv_w, conv_b, conv_g, conv_beta, w_out, ln_g, ln_b):
    T = x.shape[0]
    tok = lambda n: pl.BlockSpec((TM_CONV, n), lambda i: (i, 0))
    prev, nxt = _halo_specs(T, TM_CONV, CONV_WIDTH)
    consts = (conv_w, conv_b, conv_g, conv_beta, w_out, ln_g, ln_b)
    return pl.pallas_call(
        _odd_out_kernel,
        grid=(T // TM_CONV,),
        in_specs=[tok(MLA_WIDTH), prev, tok(CONV_WIDTH), nxt, tok(D_MODEL)]
                 + [_const_spec(a.shape) for a in consts],
        out_specs=tok(D_MODEL),
        out_shape=jax.ShapeDtypeStruct((T, D_MODEL), F32),
        scratch_shapes=[pltpu.VMEM((TM_CONV + 2 * HALO, CONV_WIDTH), F32), pltpu.VMEM((TM_CONV, CONV_WIDTH), F32),
                        pltpu.VMEM((7, TM_CONV + HALO + 8, CONV_WIDTH), F32)],
        compiler_params=_params(),
        name="odd_out",
    )(c, d, d, d, x, *consts)


def _conv_ffn_kernel(xp_ref, x_ref, xn_ref, wu_ref, cw_ref, cb_ref, wd_ref, g_ref, b_ref, o_ref,
                     lhs, h_a, h_b, acc):
    _fill_halo_buffer(lhs, xp_ref, x_ref, xn_ref, BF16)

    n_blocks = TM_FFN // FF_ROWS
    margin = 8
    total = TM_FFN + 2 * HALO
    up_rows = [0] + [HALO + b * FF_ROWS for b in range(1, n_blocks)] + [total]

    def up(j, blk, h_scr):
        rows = slice(up_rows[blk], up_rows[blk + 1])
        h_scr[rows, :] = _dot(lhs[rows, :], wu_ref[j])

    def down(j, blk, h_scr, first=False):
        cw = cw_ref[j]
        lo = HALO + blk * FF_ROWS - margin
        n = FF_ROWS + 2 * margin
        h = h_scr[lo:lo + n, :]
        conv = (pltpu.roll(h, 1, 0) * cw[0:1, :] + h * cw[1:2, :] + pltpu.roll(h, n - 1, 0) * cw[2:3, :]
                + cb_ref[j])[margin:margin + FF_ROWS, :]
        gate = conv[:, :FF_CHUNK]
        act = (gate * (1.0 / (1.0 + jnp.exp2(gate))) * conv[:, FF_CHUNK:]).astype(BF16)
        part = _dot(act, wd_ref[j])
        rows = slice(blk * FF_ROWS, (blk + 1) * FF_ROWS)
        if first:
            acc[rows, :] = part
        else:
            acc[rows, :] += part

    def step(j, cur, nxt, first=False):
        for blk in range(n_blocks):
            if nxt is not None:
                up(j + 1, blk, nxt)
            down(j, blk, cur, first)

    for blk in range(n_blocks):
        up(0, blk, h_a)
    step(0, h_a, h_b, first=True)

    def pair(i, carry):
        j = 2 * i + 1
        step(j, h_b, h_a)
        step(j + 1, h_a, h_b)
        return carry

    assert N_FF_CHUNKS % 2 == 1 and N_FF_CHUNKS >= 3
    lax.fori_loop(0, (N_FF_CHUNKS - 3) // 2, pair, 0)
    step(N_FF_CHUNKS - 2, h_b, h_a)
    step(N_FF_CHUNKS - 1, h_a, None)
    o_ref[...] = _layernorm(ALPHA * x_ref[...] + acc[...], g_ref[...], b_ref[...])


def _conv_ffn(x, w_up, conv_w, conv_b, w_down, ln_g, ln_b):
    T = x.shape[0]
    tok = pl.BlockSpec((TM_FFN, D_MODEL), lambda i: (i, 0))
    prev, nxt = _halo_specs(T, TM_FFN, D_MODEL)
    consts = (w_up, conv_w, conv_b, w_down, ln_g, ln_b)
    h_buf = pltpu.VMEM((TM_FFN + 2 * HALO, 2 * FF_CHUNK), F32)
    return pl.pallas_call(
        _conv_ffn_kernel,
        grid=(T // TM_FFN,),
        in_specs=[prev, tok, nxt] + [_const_spec(a.shape) for a in consts],
        out_specs=tok,
        out_shape=jax.ShapeDtypeStruct((T, D_MODEL), F32),
        scratch_shapes=[pltpu.VMEM((TM_FFN + 2 * HALO, D_MODEL), BF16), h_buf, h_buf,
                        pltpu.VMEM((TM_FFN, D_MODEL), F32)],
        compiler_params=_params(),
        name="conv_ffn",
    )(x, x, x, *consts)


def _row(v):
    return v.reshape(1, -1).astype(F32)


def _nbr_bias(rpb):
    c = np.arange(GRID_W)
    c0 = np.clip(c - NA_WIN_C // 2, 0, GRID_W - NA_WIN_C)
    col_ok = (c[None, :] >= c0[:, None]) & (c[None, :] < c0[:, None] + NA_WIN_C)
    dc = np.clip(c[None, :] - c[:, None], -(NA_WIN_C - 1), NA_WIN_C - 1) + NA_WIN_C - 1
    row_tab = jnp.where(col_ok[None, None], rpb[:, :, dc] * LOG2E, NEG)
    row_tab = jnp.concatenate([row_tab, jnp.full_like(row_tab[:, :1], NEG)], axis=1)
    masked = 2 * NA_WIN_R - 1
    sel = np.full((3, NA_QROWS, NA_KROWS), masked)
    for cls, m in enumerate((0, 1, NA_BLOCKS - 1)):
        key_row0 = min(max(NA_QROWS * m - NA_WIN_R // 2, 0), ROWS - NA_KROWS)
        for i in range(NA_QROWS):
            qr = NA_QROWS * m + i
            r0 = min(max(qr - NA_WIN_R // 2, 0), ROWS - NA_WIN_R)
            for j in range(NA_KROWS):
                kr = key_row0 + j
                if r0 <= kr < r0 + NA_WIN_R:
                    sel[cls, i, j] = kr - qr + NA_WIN_R - 1
    b = row_tab[:, sel]
    b = b.transpose(1, 0, 3, 5, 2, 4)
    return b.reshape(3, NA_HEADS, NA_KBLK, NA_QBLK).astype(F32)


def _rope_tables():
    pos = jnp.arange(SEQ, dtype=F32)
    inv = ROPE_THETA ** (-jnp.arange(0, MLA_ROPE, 2, dtype=F32) / MLA_ROPE)
    ang = pos[:, None] * inv[None, :]
    cos, sin = jnp.cos(ang), jnp.sin(ang)
    csq = jnp.concatenate([jnp.tile(cos, (1, 4)), jnp.tile(sin, (1, 4))], axis=1)
    csk = jnp.concatenate([jnp.tile(cos, (1, 2)), jnp.tile(sin, (1, 2)),
                           jnp.zeros((SEQ, LANES - 2 * MLA_ROPE), F32)], axis=1)
    return csq, csk


def _rot_cols(w):
    half = w.shape[-1] // 2
    return jnp.concatenate([-w[..., half:], w[..., :half]], axis=-1)


def _odd_weights(w_in, w_uq, w_ukv):
    o1 = MLA_Q_LORA
    o2 = o1 + MLA_KV_LORA
    o3 = o2 + MLA_ROPE
    k_r = w_in[:, o2:o3]
    k_blk = jnp.concatenate([k_r, _rot_cols(k_r), jnp.zeros((D_MODEL, LANES - 2 * MLA_ROPE), F32)], axis=1)
    w_in2 = jnp.concatenate([w_in[:, :o2], k_blk, w_in[:, o3:]], axis=1)

    scale = (MLA_NOPE + MLA_ROPE) ** -0.5 * LOG2E
    wq =(w_uq * scale).reshape(MLA_Q_LORA, MLA_HEADS, MLA_NOPE + MLA_ROPE)
    nope, rope = wq[:, :, :MLA_NOPE], wq[:, :, MLA_NOPE:]
    rot = _rot_cols(rope)
    blocks = []
    for p in range(MLA_HEADS // 2):
        h0, h1 = 2 * p, 2 * p + 1
        blocks += [nope[:, h0], nope[:, h1], rope[:, h0], rope[:, h1], rot[:, h0], rot[:, h1]]
    wq2 = jnp.concatenate(blocks, axis=1)

    wkv = w_ukv.reshape(MLA_KV_LORA, MLA_HEADS, MLA_NOPE + MLA_V)
    wkv2 = jnp.concatenate([wkv[:, :, :MLA_NOPE].reshape(MLA_KV_LORA, -1),
                            wkv[:, :, MLA_NOPE:].reshape(MLA_KV_LORA, -1)], axis=1)
    return w_in2.astype(BF16), wq2.astype(BF16), wkv2.astype(BF16)


def _ffn_weights(w_up, conv_w, conv_b, w_down):
    def chunked(a):
        g = a[..., :D_FF].reshape(a.shape[:-1] + (N_FF_CHUNKS, FF_CHUNK))
        v = a[..., D_FF:].reshape(a.shape[:-1] + (N_FF_CHUNKS, FF_CHUNK))
        return jnp.moveaxis(jnp.concatenate([g, v], axis=-1), -2, 0)
    col_scale = jnp.concatenate([jnp.full((D_FF,), -LOG2E, F32), jnp.full((D_FF,), -1.0 / LOG2E, F32)])
    return (chunked(w_up).astype(BF16), chunked(conv_w * col_scale).astype(F32),
            chunked((conv_b * col_scale)[None, :]).astype(F32),
            w_down.reshape(N_FF_CHUNKS, FF_CHUNK, D_MODEL).astype(BF16))


def _trunk(xs, w_in_even, rpb, sgu_ln_g, sgu_ln_b, sgu_w, sgu_b, w_out_even,
           w_in_odd, q_norm_g, w_uq, kv_norm_g, w_ukv, conv_w, conv_b, conv_ln_g, conv_ln_b,
           w_out_odd, ffn_w_up, ffn_conv_w, ffn_conv_b, ffn_w_down, ln1_g, ln1_b, ln2_g, ln2_b):
    shapes = [x.shape for x in xs]
    xs = [x.reshape(-1, D_MODEL) for x in xs]
    per_seq = lambda a: a.reshape(-1, SEQ, a.shape[-1])
    csq, csk = _rope_tables()
    for l in range(DEPTH):
        i = l // 2
        ln1 = (_row(ln1_g[l]), _row(ln1_b[l]))
        if l % 2 == 0:
            w_in = jnp.concatenate([w_in_even[i][:, :NA_WIDTH] * (NA_HEAD_DIM ** -0.5 * LOG2E),
                                    w_in_even[i][:, NA_WIDTH:]], axis=1).astype(BF16)
            sgu_ln = (_row(sgu_ln_g[i]), _row(sgu_ln_b[i]))
            bias = _nbr_bias(rpb[i])
            sgu_bias = jnp.broadcast_to(sgu_b[i][:, :, None], (SGU_GROUPS, SGU_CHUNK, LANES)).astype(F32)
            sw, w_out = sgu_w[i].astype(BF16), w_out_even[i].astype(BF16)

            def mixer(x):
                qkv, u, v = _even_in(x, w_in, *sgu_ln)
                a = _nbr_attn(per_seq(qkv), bias).reshape(-1, NA_WIDTH)
                return _even_out(a, u, v, x, sw, sgu_bias, w_out, *ln1)
        else:
            w_in, wq, wkv = _odd_weights(w_in_odd[i], w_uq[i], w_ukv[i])
            norms = (_row(q_norm_g[i]), _row(kv_norm_g[i]))
            cw = jnp.concatenate([conv_w[i], jnp.zeros((1, CONV_WIDTH), F32)], axis=0)
            conv = (cw, _row(conv_b[i]), _row(conv_ln_g[i]), _row(conv_ln_b[i]))
            w_out = w_out_odd[i].astype(BF16)

            def mixer(x):
                q, kn, kr, v, d = _odd_in(x, w_in, norms[0], wq, norms[1], wkv, csq, csk)
                c = _latent_attn(per_seq(q), per_seq(kn), per_seq(kr), per_seq(v)).reshape(-1, MLA_WIDTH)
                return _odd_out(c, d, x, *conv, w_out, *ln1)
        ffn = _ffn_weights(ffn_w_up[l], ffn_conv_w[l], ffn_conv_b[l], ffn_w_down[l])
        xs = [_conv_ffn(mixer(x), *ffn, _row(ln2_g[l]), _row(ln2_b[l])) for x in xs]
    return tuple(x.reshape(shape) for x, shape in zip(xs, shapes))


def kernel(x_prompt, x_sample, w_in_even, rpb, sgu_ln_g, sgu_ln_b, sgu_w, sgu_b, w_out_even, w_in_odd, q_norm_g, w_uq, kv_norm_g, w_ukv, conv_w, conv_b, conv_ln_g, conv_ln_b, w_out_odd, ffn_w_up, ffn_conv_w, ffn_conv_b, ffn_w_down, ln1_g, ln1_b, ln2_g, ln2_b):
    return _trunk([x_prompt, x_sample], w_in_even, rpb, sgu_ln_g, sgu_ln_b, sgu_w, sgu_b, w_out_even,
                  w_in_odd, q_norm_g, w_uq, kv_norm_g, w_ukv, conv_w, conv_b, conv_ln_g, conv_ln_b,
                  w_out_odd, ffn_w_up, ffn_conv_w, ffn_conv_b, ffn_w_down, ln1_g, ln1_b, ln2_g, ln2_b)
```
